```python
import math
import jax, jax.numpy as jnp
from jax import lax
import numpy as np

D_MODEL = 1024
BATCH = 32
SEQ = 2048
DEPTH = 1

N_MEM = 256
D_MIX = D_MODEL
CHUNK = 128
GM_GROUPS = 4
GM_WIDTH = D_MIX // 2
GM_DIM = GM_WIDTH // GM_GROUPS
SB_WIDTH = D_MIX - GM_WIDTH
SB_HEADS = 8
SB_HEAD_DIM = SB_WIDTH // SB_HEADS
Q_BLOCK = 128
IN_COLS = 2 * GM_WIDTH + 3 * SB_WIDTH
X_HEADS = 4
X_HEAD_DIM = D_MODEL // X_HEADS
D_FF = 4 * D_MODEL
EPS = 1e-6

kernel_name = "hybrid_gmlp_stickbreaking_memxattn_block"


def rms_norm(x, g):
    xf = x.astype(jnp.float32)
    y = xf * lax.rsqrt(jnp.mean(xf * xf, axis=-1, keepdims=True) + EPS)
    return (y * g.astype(jnp.float32)).astype(x.dtype)


def head_rms(x):
    xf = x.astype(jnp.float32)
    return (xf * lax.rsqrt(jnp.mean(xf * xf, axis=-1, keepdims=True) + EPS)).astype(x.dtype)


def spatial_gating(u, v, v_norm_g, w_spatial, b_spatial):
    b, s, _ = u.shape
    n_chunks = s // CHUNK
    u = u.reshape(b, n_chunks, CHUNK, GM_GROUPS, GM_DIM)
    v = v.reshape(b, n_chunks, CHUNK, GM_GROUPS, GM_DIM)
    v = rms_norm(v, v_norm_g.reshape(GM_GROUPS, GM_DIM))
    mask = jnp.tril(jnp.ones((CHUNK, CHUNK), dtype=bool))
    w = jnp.where(mask[None], w_spatial, jnp.zeros_like(w_spatial))
    mixed = jnp.einsum('gts,bcsgd->bctgd', w, v) + b_spatial.T[None, None, :, :, None]
    return u * mixed


def stick_breaking_attention(q, k, v):
    s_len = q.shape[2]
    scale = 1.0 / math.sqrt(SB_HEAD_DIM)
    outs = []
    for blk in range(s_len // Q_BLOCK):
        q0 = blk * Q_BLOCK
        k_end = q0 + Q_BLOCK
        qb = q[:, :, q0:k_end]
        kb = k[:, :, :k_end]
        vb = v[:, :, :k_end]
        z = jnp.einsum('bhtd,bhsd->bhts', qb, kb, preferred_element_type=jnp.float32) * scale
        t_idx = q0 + jnp.arange(Q_BLOCK)[:, None]
        s_idx = jnp.arange(k_end)[None, :]
        causal = s_idx < t_idx
        log_beta = jax.nn.log_sigmoid(z)
        log_1m = jnp.where(causal, jax.nn.log_sigmoid(-z), 0.0)
        cum = jnp.cumsum(log_1m, axis=-1)
        total = cum[..., -1:]
        log_a = log_beta + (total - cum)
        a = jnp.where(causal, jnp.exp(log_a), 0.0)
        outs.append(jnp.einsum('bhts,bhsd->bhtd', a.astype(vb.dtype), vb))
    return jnp.concatenate(outs, axis=2)


def mixer(xn, w_in, gm_v_norm_g, w_spatial, b_spatial, head_norm_g, w_out):
    b, s, _ = xn.shape
    proj = xn @ w_in
    u = jax.nn.gelu(proj[..., :GM_WIDTH])
    gv = jax.nn.gelu(proj[..., GM_WIDTH:2 * GM_WIDTH])
    qkv = proj[..., 2 * GM_WIDTH:].reshape(b, s, 3, SB_HEADS, SB_HEAD_DIM)
    q = qkv[:, :, 0].transpose(0, 2, 1, 3)
    k = qkv[:, :, 1].transpose(0, 2, 1, 3)
    v = qkv[:, :, 2].transpose(0, 2, 1, 3)
    a_out = head_rms(spatial_gating(u, gv, gm_v_norm_g, w_spatial, b_spatial)).reshape(b, s, GM_WIDTH)
    sb = stick_breaking_attention(q, k, v).transpose(0, 2, 1, 3)
    b_out = head_rms(sb).reshape(b, s, SB_WIDTH)
    merged = jnp.concatenate([a_out, b_out], axis=-1) * head_norm_g
    return merged @ w_out


def cross_attention(hn, mem_n, w_cq, w_ckv, w_co):
    b, s, _ = hn.shape
    m = mem_n.shape[1]
    q = (hn @ w_cq).reshape(b, s, X_HEADS, X_HEAD_DIM)
    kv = (mem_n @ w_ckv).reshape(b, m, 2, X_HEADS, X_HEAD_DIM)
    k, v = kv[:, :, 0], kv[:, :, 1]
    scores = jnp.einsum('bshd,bmhd->bhsm', q, k, preferred_element_type=jnp.float32) / math.sqrt(X_HEAD_DIM)
    p = jax.nn.softmax(scores, axis=-1)
    o = jnp.einsum('bhsm,bmhd->bshd', p.astype(v.dtype), v).reshape(b, s, D_MODEL)
    return o @ w_co


def sq_relu_mlp(hn, w_ff1, w_ff2):
    h = jax.nn.relu(hn @ w_ff1)
    return (h * h) @ w_ff2


def setup_inputs(seed: int = 0) -> dict:
    key = jax.random.key(seed)
    ks = jax.random.split(key, 20)
    f32 = jnp.float32

    def nrm(k, shape, scale):
        return jax.random.normal(k, shape, f32) * scale

    def gain(k, shape):
        return 1.0 + 0.02 * jax.random.normal(k, shape, f32)

    L = DEPTH
    return {
        "x": jax.random.normal(ks[0], (BATCH, SEQ, D_MODEL), f32),
        "mem": jax.random.normal(ks[1], (BATCH, N_MEM, D_MODEL), f32),
        "norm_mix_g": gain(ks[2], (L, D_MODEL)),
        "w_in": nrm(ks[3], (L, D_MODEL, IN_COLS), D_MODEL ** -0.5),
        "gm_v_norm_g": gain(ks[4], (L, GM_WIDTH)),
        "w_spatial": nrm(ks[5], (L, GM_GROUPS, CHUNK, CHUNK), 0.5 * CHUNK ** -0.5),
        "b_spatial": 1.0 + 0.01 * jax.random.normal(ks[6], (L, GM_GROUPS, CHUNK), f32),
        "head_norm_g": gain(ks[7], (L, D_MIX)),
        "w_out": nrm(ks[8], (L, D_MIX, D_MODEL), D_MIX ** -0.5),
        "norm_cross_g": gain(ks[9], (L, D_MODEL)),
        "norm_mem_g": gain(ks[10], (L, D_MODEL)),
        "w_cq": nrm(ks[11], (L, D_MODEL, D_MODEL), D_MODEL ** -0.5),
        "w_ckv": nrm(ks[12], (L, D_MODEL, 2 * D_MODEL), D_MODEL ** -0.5),
        "w_co": nrm(ks[13], (L, D_MODEL, D_MODEL), D_MODEL ** -0.5),
        "norm_ffn_g": gain(ks[14], (L, D_MODEL)),
        "w_ff1": nrm(ks[15], (L, D_MODEL, D_FF), D_MODEL ** -0.5),
        "w_ff2": nrm(ks[16], (L, D_FF, D_MODEL), D_FF ** -0.5),
        "norm_final_g": gain(ks[17], (D_MODEL,)),
    }


def reference(x, mem, norm_mix_g, w_in, gm_v_norm_g, w_spatial, b_spatial, head_norm_g, w_out,
              norm_cross_g, norm_mem_g, w_cq, w_ckv, w_co, norm_ffn_g, w_ff1, w_ff2, norm_final_g):
    h = x
    for l in range(DEPTH):
        xn = rms_norm(h, norm_mix_g[l])
        h = h + mixer(xn, w_in[l], gm_v_norm_g[l], w_spatial[l], b_spatial[l], head_norm_g[l], w_out[l])
        hn = rms_norm(h, norm_cross_g[l])
        mem_n = rms_norm(mem, norm_mem_g[l])
        h = h + cross_attention(hn, mem_n, w_cq[l], w_ckv[l], w_co[l])
        hn = rms_norm(h, norm_ffn_g[l])
        h = h + sq_relu_mlp(hn, w_ff1[l], w_ff2[l])
    return rms_norm(h, norm_final_g)
```

```python
import functools
import math

import jax
import jax.numpy as jnp
from jax import lax
from jax.experimental import pallas as pl
from jax.experimental.pallas import tpu as pltpu

EPS = 1e-6
CHUNK = 128
GM_GROUPS = 4
GM_DIM = 128
GM_WIDTH = GM_GROUPS * GM_DIM
SB_HEADS = 8
SB_HEAD_DIM = 64
SB_WIDTH = SB_HEADS * SB_HEAD_DIM
SB_BLOCK = 128
X_HEADS = 4

LANES = 128
VMEM_LIMIT = 56 * 1024 * 1024

LOG_ZERO = -104.0

F32 = jnp.float32
BF16 = jnp.bfloat16


def _dot(a, b):
    return jnp.dot(a, b, preferred_element_type=F32)


def _dot_nt(a, b):
    return lax.dot_general(a, b, (((1,), (1,)), ((), ())), preferred_element_type=F32)


def _rms(x, g):
    return x * lax.rsqrt(jnp.mean(x * x, axis=-1, keepdims=True) + EPS) * g


def _resident(shape):
    zeros = (0,) * len(shape)
    return pl.BlockSpec(shape, lambda *_: zeros, pipeline_mode=pl.Buffered(1))


def _mix_in_kernel(x_ref, g_ref, win_ref, gvg_ref, wsp_ref, bsp_ref, hg_ref,
                   a_ref, q_ref, k_ref, v_ref, *, tm):
    xn = _rms(x_ref[...], g_ref[...]).astype(BF16)
    u = jax.nn.gelu(_dot(xn, win_ref[:, 0:GM_WIDTH]))
    gv = jax.nn.gelu(_dot(xn, win_ref[:, GM_WIDTH:2 * GM_WIDTH]))
    for g in range(GM_GROUPS):
        cols = slice(g * GM_DIM, (g + 1) * GM_DIM)
        gvn = _rms(gv[:, cols], gvg_ref[:, cols]).astype(BF16)
        for c in range(tm // CHUNK):
            rows = slice(c * CHUNK, (c + 1) * CHUNK)
            mixed = _dot(wsp_ref[g], gvn[rows]) + bsp_ref[g]
            gated = u[rows, cols] * mixed
            a_ref[rows, cols] = _rms(gated, hg_ref[:, cols]).astype(BF16)
    base = 2 * GM_WIDTH
    scale = 1.0 / math.sqrt(SB_HEAD_DIM)
    q_ref[...] = (_dot(xn, win_ref[:, base:base + SB_WIDTH]) * scale).astype(BF16)
    k_ref[...] = _dot(xn, win_ref[:, base + SB_WIDTH:base + 2 * SB_WIDTH]).astype(BF16)
    v_ref[...] = _dot(xn, win_ref[:, base + 2 * SB_WIDTH:base + 3 * SB_WIDTH]).astype(BF16)


def _mix_in(x, g, w_in, gvg, wsp, bsp, hg, *, tm=256):
    b, s, d = x.shape
    tok = lambda width: pl.BlockSpec((None, tm, width), lambda i, j: (i, j, 0))
    out = jax.ShapeDtypeStruct((b, s, GM_WIDTH), BF16)
    return pl.pallas_call(
        functools.partial(_mix_in_kernel, tm=tm),
        grid=(b, s // tm),
        in_specs=[tok(d), _resident(g.shape), _resident(w_in.shape), _resident(gvg.shape),
                  _resident(wsp.shape), _resident(bsp.shape), _resident(hg.shape)],
        out_specs=[tok(GM_WIDTH)] * 4,
        out_shape=[out] * 4,
        compiler_params=pltpu.CompilerParams(
            dimension_semantics=("parallel", "parallel"), vmem_limit_bytes=VMEM_LIMIT),
        name="mix_in",
    )(x, g, w_in, gvg, wsp, bsp, hg)


def _sb_attn_kernel(q_ref, k_ref, v_ref, tri_ref, hg_ref, o_ref, r_ref, acc_ref):
    qi = pl.program_id(2)
    lane = lax.broadcasted_iota(jnp.int32, (SB_BLOCK, LANES), 1)
    first_head = lane < SB_HEAD_DIM
    q2 = q_ref[...]
    zero = jnp.zeros_like(q2)
    qs = jnp.concatenate([jnp.where(first_head, q2, zero), jnp.where(first_head, zero, q2)], axis=0)

    t_idx = lax.broadcasted_iota(jnp.int32, (2 * SB_BLOCK, SB_BLOCK), 0) % SB_BLOCK
    s_idx = lax.broadcasted_iota(jnp.int32, (2 * SB_BLOCK, SB_BLOCK), 1)
    strictly_causal = s_idx < t_idx

    def block(j, masked):
        start = pl.multiple_of(j * SB_BLOCK, SB_BLOCK)
        kb = k_ref[pl.ds(start, SB_BLOCK), :]
        vb = v_ref[pl.ds(start, SB_BLOCK), :]
        z = _dot_nt(qs, kb)
        log_beta = jnp.minimum(z, 0.0) - jnp.log(1.0 + jnp.exp(-jnp.abs(z)))
        log_1m = log_beta - z
        if masked:
            log_1m = jnp.where(strictly_causal, log_1m, 0.0)
        hi = log_1m.astype(BF16)
        lo = (log_1m - hi.astype(F32)).astype(BF16)
        sums = _dot(jnp.concatenate([hi, lo], axis=1), tri_ref[...])
        r = r_ref[...]
        a = jnp.exp(log_beta + sums[:, :SB_BLOCK] + r)
        if masked:
            a = jnp.where(strictly_causal, a, 0.0)
        a = a.astype(BF16)
        a2 = jnp.concatenate([a[:SB_BLOCK], a[SB_BLOCK:]], axis=1)
        zv = jnp.zeros_like(vb)
        vv = jnp.concatenate([jnp.where(first_head, vb, zv), jnp.where(first_head, zv, vb)], axis=0)
        acc_ref[...] += _dot(a2, vv)
        r_new = r + sums[:, SB_BLOCK:]
        r_ref[...] = r_new
        return jnp.max(r_new)

    r_ref[...] = jnp.zeros_like(r_ref)
    acc_ref[...] = jnp.zeros_like(acc_ref)
    bound = block(qi, True)

    def cond(carry):
        j, bound = carry
        return jnp.logical_and(j >= 0, bound >= LOG_ZERO)

    def body(carry):
        j, _ = carry
        return j - 1, block(j, False)

    lax.while_loop(cond, body, (qi - 1, bound))

    acc = acc_ref[...]
    sq = acc * acc
    ms0 = jnp.sum(jnp.where(first_head, sq, 0.0), axis=-1, keepdims=True)
    ms1 = jnp.sum(jnp.where(first_head, 0.0, sq), axis=-1, keepdims=True)
    ms = jnp.where(first_head, ms0, ms1) * (1.0 / SB_HEAD_DIM)
    o_ref[...] = (acc * lax.rsqrt(ms + EPS) * hg_ref[...]).astype(BF16)


def _sb_attn(q, k, v, tri, hg):
    b, s, _ = q.shape
    pairs = SB_WIDTH // LANES
    blk = pl.BlockSpec((None, SB_BLOCK, LANES), lambda i, p, j: (i, j, p))
    seq = pl.BlockSpec((None, s, LANES), lambda i, p, j: (i, 0, p))
    return pl.pallas_call(
        _sb_attn_kernel,
        grid=(b, pairs, s // SB_BLOCK),
        in_specs=[blk, seq, seq, _resident(tri.shape),
                  pl.BlockSpec((1, LANES), lambda i, p, j: (0, p))],
        out_specs=blk,
        out_shape=jax.ShapeDtypeStruct((b, s, SB_WIDTH), BF16),
        scratch_shapes=[pltpu.VMEM((2 * SB_BLOCK, SB_BLOCK), F32),
                        pltpu.VMEM((SB_BLOCK, LANES), F32)],
        compiler_params=pltpu.CompilerParams(
            dimension_semantics=("parallel", "parallel", "arbitrary"),
            vmem_limit_bytes=VMEM_LIMIT),
        name="sb_attn",
    )(q, k, v, tri, hg)


def _mem_kv_kernel(mem_ref, g_ref, w_ref, kt_ref, v_ref):
    d = mem_ref.shape[-1]
    mn = _rms(mem_ref[...], g_ref[...]).astype(BF16)
    kt_ref[...] = _dot(mn, w_ref[:, 0:d]).T.astype(BF16)
    v_ref[...] = _dot(mn, w_ref[:, d:2 * d]).astype(BF16)


def _mem_kv(mem, g, w_ckv):
    b, m, d = mem.shape
    return pl.pallas_call(
        _mem_kv_kernel,
        grid=(b,),
        in_specs=[pl.BlockSpec((None, m, d), lambda i: (i, 0, 0)),
                  _resident(g.shape), _resident(w_ckv.shape)],
        out_specs=[pl.BlockSpec((None, d, m), lambda i: (i, 0, 0)),
                   pl.BlockSpec((None, m, d), lambda i: (i, 0, 0))],
        out_shape=[jax.ShapeDtypeStruct((b, d, m), BF16), jax.ShapeDtypeStruct((b, m, d), BF16)],
        compiler_params=pltpu.CompilerParams(
            dimension_semantics=("parallel",), vmem_limit_bytes=VMEM_LIMIT),
        name="mem_kv",
    )(mem, g, w_ckv)


def _tail_kernel(x_ref, a_ref, b_ref, kt_ref, v_ref, wout_ref, gc_ref, wcq_ref, wco_ref,
                 gf_ref, w1_ref, w2_ref, gfin_ref, o_ref, *, ff_chunk):
    d = x_ref.shape[-1]
    hd = d // X_HEADS
    merged = jnp.concatenate([a_ref[...], b_ref[...]], axis=1)
    h = x_ref[...] + _dot(merged, wout_ref[...])

    hn = _rms(h, gc_ref[...]).astype(BF16)
    qx = (_dot(hn, wcq_ref[...]) * (1.0 / math.sqrt(hd))).astype(BF16)
    heads = []
    for i in range(X_HEADS):
        cols = slice(i * hd, (i + 1) * hd)
        sc = _dot(qx[:, cols], kt_ref[cols, :])
        e = jnp.exp(sc - jnp.max(sc, axis=-1, keepdims=True))
        p = e / jnp.sum(e, axis=-1, keepdims=True)
        heads.append(_dot(p.astype(BF16), v_ref[:, cols]).astype(BF16))
    h = h + _dot(jnp.concatenate(heads, axis=1), wco_ref[...])

    hn = _rms(h, gf_ref[...]).astype(BF16)
    d_ff = w1_ref.shape[1]
    ff = None
    for c in range(d_ff // ff_chunk):
        cols = slice(c * ff_chunk, (c + 1) * ff_chunk)
        r = jnp.maximum(_dot(hn, w1_ref[:, cols]), 0.0)
        part = _dot((r * r).astype(BF16), w2_ref[cols, :])
        ff = part if ff is None else ff + part
    o_ref[...] = _rms(h + ff, gfin_ref[...])


def _tail(x, a, bo, kt, vm, w_out, gc, w_cq, w_co, gf, w1, w2, gfin, *, tm=256, ff_chunk=1024):
    b, s, d = x.shape
    m = vm.shape[1]
    tok = lambda width: pl.BlockSpec((None, tm, width), lambda i, j: (i, j, 0))
    return pl.pallas_call(
        functools.partial(_tail_kernel, ff_chunk=ff_chunk),
        grid=(b, s // tm),
        in_specs=[tok(d), tok(GM_WIDTH), tok(SB_WIDTH),
                  pl.BlockSpec((None, d, m), lambda i, j: (i, 0, 0)),
                  pl.BlockSpec((None, m, d), lambda i, j: (i, 0, 0)),
                  _resident(w_out.shape), _resident(gc.shape), _resident(w_cq.shape),
                  _resident(w_co.shape), _resident(gf.shape), _resident(w1.shape),
                  _resident(w2.shape), _resident(gfin.shape)],
        out_specs=tok(d),
        out_shape=jax.ShapeDtypeStruct((b, s, d), x.dtype),
        compiler_params=pltpu.CompilerParams(
            dimension_semantics=("parallel", "parallel"), vmem_limit_bytes=VMEM_LIMIT),
        name="tail",
    )(x, a, bo, kt, vm, w_out, gc, w_cq, w_co, gf, w1, w2, gfin)


def _suffix_sum_matrix():
    j = lax.broadcasted_iota(jnp.int32, (SB_BLOCK, SB_BLOCK), 0)
    s = lax.broadcasted_iota(jnp.int32, (SB_BLOCK, SB_BLOCK), 1)
    half = jnp.concatenate([(j > s).astype(BF16), jnp.ones((SB_BLOCK, SB_BLOCK), BF16)], axis=1)
    return jnp.concatenate([half, half], axis=0)


def kernel(x, mem, norm_mix_g, w_in, gm_v_norm_g, w_spatial, b_spatial, head_norm_g, w_out,
           norm_cross_g, norm_mem_g, w_cq, w_ckv, w_co, norm_ffn_g, w_ff1, w_ff2, norm_final_g):
    assert w_in.shape[0] == 1, "single trunk layer only"
    tril = jnp.tril(jnp.ones((CHUNK, CHUNK), dtype=bool))
    row = lambda g: g.reshape(1, -1).astype(F32)
    wsp = jnp.where(tril[None], w_spatial[0], 0.0).astype(BF16)
    bsp = jnp.broadcast_to(b_spatial[0][:, :, None], (GM_GROUPS, CHUNK, GM_DIM)).astype(F32)
    hg = row(head_norm_g[0])
    a, q, k, v = _mix_in(x, row(norm_mix_g[0]), w_in[0].astype(BF16), row(gm_v_norm_g[0]),
                         wsp, bsp, hg[:, :GM_WIDTH])
    sb = _sb_attn(q, k, v, _suffix_sum_matrix(), hg[:, GM_WIDTH:])
    kt, vm = _mem_kv(mem, row(norm_mem_g[0]), w_ckv[0].astype(BF16))
    return _tail(x, a, sb, kt, vm, w_out[0].astype(BF16), row(norm_cross_g[0]),
                 w_cq[0].astype(BF16), w_co[0].astype(BF16), row(norm_ffn_g[0]),
                 w_ff1[0].astype(BF16), w_ff2[0].astype(BF16), row(norm_final_g))
```

```python
import functools
import math

import jax
import jax.numpy as jnp
from jax import lax
from jax.experimental import pallas as pl
from jax.experimental.pallas import tpu as pltpu

EPS = 1e-6
CHUNK = 128
GM_GROUPS = 4
GM_DIM = 128
GM_WIDTH = GM_GROUPS * GM_DIM
SB_HEADS = 8
SB_HEAD_DIM = 64
SB_WIDTH = SB_HEADS * SB_HEAD_DIM
SB_BLOCK = 128
SB_QBLOCKS = 2
X_HEADS = 4

LANES = 128
VMEM_LIMIT = 56 * 1024 * 1024

LOG2_ZERO = -151.0
LOG2_E = 1.4426950408889634

F32 = jnp.float32
BF16 = jnp.bfloat16


def _dot(a, b):
    return jnp.dot(a, b, preferred_element_type=F32)


def _dot_nt(a, b):
    return lax.dot_general(a, b, (((1,), (1,)), ((), ())), preferred_element_type=F32)


def _rms(x, g):
    return x * lax.rsqrt(jnp.mean(x * x, axis=-1, keepdims=True) + EPS) * g


def _resident(shape):
    zeros = (0,) * len(shape)
    return pl.BlockSpec(shape, lambda *_: zeros, pipeline_mode=pl.Buffered(1))


def _mix_in_kernel(x_ref, g_ref, win_ref, gvg_ref, wsp_ref, bsp_ref, hg_ref,
                   a_ref, q_ref, k_ref, vt_ref, *, tm):
    xn = _rms(x_ref[...], g_ref[...]).astype(BF16)
    u = jax.nn.gelu(_dot(xn, win_ref[:, 0:GM_WIDTH]))
    gv = jax.nn.gelu(_dot(xn, win_ref[:, GM_WIDTH:2 * GM_WIDTH]))
    for g in range(GM_GROUPS):
        cols = slice(g * GM_DIM, (g + 1) * GM_DIM)
        gvn = _rms(gv[:, cols], gvg_ref[:, cols]).astype(BF16)
        for c in range(tm // CHUNK):
            rows = slice(c * CHUNK, (c + 1) * CHUNK)
            mixed = _dot(wsp_ref[g], gvn[rows]) + bsp_ref[g]
            gated = u[rows, cols] * mixed
            a_ref[rows, cols] = _rms(gated, hg_ref[:, cols]).astype(BF16)
    base = 2 * GM_WIDTH
    scale = LOG2_E / math.sqrt(SB_HEAD_DIM)
    q_ref[...] = (_dot(xn, win_ref[:, base:base + SB_WIDTH]) * scale).astype(BF16)
    k_ref[...] = _dot(xn, win_ref[:, base + SB_WIDTH:base + 2 * SB_WIDTH]).astype(BF16)
    v = _dot(xn, win_ref[:, base + 2 * SB_WIDTH:base + 3 * SB_WIDTH])
    for c in range(tm // SB_BLOCK):
        vt_ref[c] = v[c * SB_BLOCK:(c + 1) * SB_BLOCK, :].T.astype(BF16)


def _mix_in(x, g, w_in, gvg, wsp, bsp, hg, *, tm=512):
    b, s, d = x.shape
    tok = lambda width: pl.BlockSpec((None, tm, width), lambda i, j: (i, j, 0))
    out = jax.ShapeDtypeStruct((b, s, GM_WIDTH), BF16)
    vt_out = jax.ShapeDtypeStruct((b, s // SB_BLOCK, SB_WIDTH, SB_BLOCK), BF16)
    vt_spec = pl.BlockSpec((None, tm // SB_BLOCK, SB_WIDTH, SB_BLOCK), lambda i, j: (i, j, 0, 0))
    return pl.pallas_call(
        functools.partial(_mix_in_kernel, tm=tm),
        grid=(b, s // tm),
        in_specs=[tok(d), _resident(g.shape), _resident(w_in.shape), _resident(gvg.shape),
                  _resident(wsp.shape), _resident(bsp.shape), _resident(hg.shape)],
        out_specs=[tok(GM_WIDTH)] * 3 + [vt_spec],
        out_shape=[out] * 3 + [vt_out],
        compiler_params=pltpu.CompilerParams(
            dimension_semantics=("parallel", "parallel"), vmem_limit_bytes=VMEM_LIMIT),
        name="mix_in",
    )(x, g, w_in, gvg, wsp, bsp, hg)


def _sb_attn_kernel(q_ref, k_ref, vt_ref, tri_ref, hg_ref, o_ref, qbd_ref, r_ref, acc_ref):
    base = pl.program_id(1) * SB_QBLOCKS
    pairs = q_ref.shape[-1] // LANES
    cols = [slice(p * LANES, (p + 1) * LANES) for p in range(pairs)]
    top_rows = lax.broadcasted_iota(jnp.int32, (LANES, SB_BLOCK), 0) < SB_HEAD_DIM
    s_idx = lax.broadcasted_iota(jnp.int32, (SB_BLOCK, 2 * SB_BLOCK), 0)
    t_idx = lax.broadcasted_iota(jnp.int32, (SB_BLOCK, 2 * SB_BLOCK), 1) % SB_BLOCK
    strictly_causal = s_idx < t_idx

    def block_diag_heads(x):
        zero = jnp.zeros_like(x)
        return jnp.concatenate([jnp.where(top_rows, x, zero), jnp.where(top_rows, zero, x)], axis=1)

    for c in range(SB_QBLOCKS):
        q_t = q_ref[c * SB_BLOCK:(c + 1) * SB_BLOCK, :].astype(F32).T
        for p in range(pairs):
            qbd_ref[c, p] = block_diag_heads(q_t[cols[p], :].astype(BF16))

    def step(blocks, masked):
        chains = [(c, j, p) for c, j in blocks for p in range(pairs)]
        z = [_dot(k_ref[pl.ds(pl.multiple_of(j * SB_BLOCK, SB_BLOCK), SB_BLOCK), cols[p]], qbd_ref[c, p])
             for c, j, p in chains]
        log_beta, first_row, hl = [], [], []
        for zc in z:
            lb = jnp.minimum(zc, 0.0) - jnp.log2(1.0 + jnp.exp2(-jnp.abs(zc)))
            lm = lb - zc
            if masked:
                lm = jnp.where(strictly_causal, lm, 0.0)
            hi = lm.astype(BF16)
            lo = (lm - hi.astype(F32)).astype(BF16)
            log_beta.append(lb)
            first_row.append(lm[0:1, :])
            hl.append(jnp.concatenate([hi, lo], axis=0))
        suffix = [_dot(tri_ref[...], x) for x in hl]
        a_t, r_new = [], []
        for i, (c, j, p) in enumerate(chains):
            r = r_ref[c, p]
            a = jnp.exp2(log_beta[i] + suffix[i] + r)
            if masked:
                a = jnp.where(strictly_causal, a, 0.0)
            a = a.astype(BF16)
            a_t.append(jnp.concatenate([a[:, :SB_BLOCK], a[:, SB_BLOCK:]], axis=0))
            r_new.append(r + suffix[i][0:1, :] + first_row[i])
        pv = [_dot(block_diag_heads(vt_ref[j, cols[p], :]), a_t[i])
              for i, (c, j, p) in enumerate(chains)]
        bound = r_new[0]
        for i, (c, j, p) in enumerate(chains):
            acc_ref[c, p] += pv[i]
            r_ref[c, p] = r_new[i]
            bound = jnp.maximum(bound, r_new[i])
        return jnp.max(bound)

    r_ref[...] = jnp.zeros_like(r_ref)
    acc_ref[...] = jnp.zeros_like(acc_ref)
    every = range(SB_QBLOCKS)
    bound = step([(c, base + c) for c in every], True)

    def cond(carry):
        m, bound = carry
        return jnp.logical_and(m <= base, bound >= LOG2_ZERO)

    def body(carry):
        m, _ = carry
        return m + 1, step([(c, base + c - m) for c in every], False)

    _, bound = lax.while_loop(cond, body, (jnp.int32(1), bound))
    for k in range(1, SB_QBLOCKS):
        bound = lax.cond(bound >= LOG2_ZERO,
                         lambda k=k: step([(c, c - k) for c in every if c >= k], False),
                         lambda bound=bound: bound)

    for c in every:
        for p in range(pairs):
            acc = acc_ref[c, p]
            sq = acc * acc
            ms0 = jnp.sum(sq[:SB_HEAD_DIM], axis=0, keepdims=True)
            ms1 = jnp.sum(sq[SB_HEAD_DIM:], axis=0, keepdims=True)
            ms = jnp.where(top_rows, ms0, ms1) * (1.0 / SB_HEAD_DIM)
            normed = (acc * lax.rsqrt(ms + EPS)).T
            o_ref[c * SB_BLOCK:(c + 1) * SB_BLOCK, cols[p]] = (normed * hg_ref[:, cols[p]]).astype(BF16)


def _sb_attn(q, k, vt, tri, hg):
    b, s, width = q.shape
    pairs = width // LANES
    rows = SB_QBLOCKS * SB_BLOCK
    blk = pl.BlockSpec((None, rows, width), lambda i, j: (i, j, 0))
    return pl.pallas_call(
        _sb_attn_kernel,
        grid=(b, s // rows),
        in_specs=[blk, pl.BlockSpec((None, s, width), lambda i, j: (i, 0, 0)),
                  pl.BlockSpec((None,) + vt.shape[1:], lambda i, j: (i, 0, 0, 0)),
                  _resident(tri.shape), _resident(hg.shape)],
        out_specs=blk,
        out_shape=jax.ShapeDtypeStruct((b, s, width), BF16),
        scratch_shapes=[pltpu.VMEM((SB_QBLOCKS, pairs, LANES, 2 * SB_BLOCK), BF16),
                        pltpu.VMEM((SB_QBLOCKS, pairs, 1, 2 * SB_BLOCK), F32),
                        pltpu.VMEM((SB_QBLOCKS, pairs, LANES, SB_BLOCK), F32)],
        compiler_params=pltpu.CompilerParams(
            dimension_semantics=("parallel", "arbitrary"), vmem_limit_bytes=VMEM_LIMIT),
        name="sb_attn",
    )(q, k, vt, tri, hg)


def _mem_kv_kernel(mem_ref, g_ref, w_ref, kt_ref, v_ref):
    d = mem_ref.shape[-1]
    mn = _rms(mem_ref[...], g_ref[...]).astype(BF16)
    kt_ref[...] = _dot(mn, w_ref[:, 0:d]).T.astype(BF16)
    v_ref[...] = _dot(mn, w_ref[:, d:2 * d]).astype(BF16)


def _mem_kv(mem, g, w_ckv):
    b, m, d = mem.shape
    return pl.pallas_call(
        _mem_kv_kernel,
        grid=(b,),
        in_specs=[pl.BlockSpec((None, m, d), lambda i: (i, 0, 0)),
                  _resident(g.shape), _resident(w_ckv.shape)],
        out_specs=[pl.BlockSpec((None, d, m), lambda i: (i, 0, 0)),
                   pl.BlockSpec((None, m, d), lambda i: (i, 0, 0))],
        out_shape=[jax.ShapeDtypeStruct((b, d, m), BF16), jax.ShapeDtypeStruct((b, m, d), BF16)],
        compiler_params=pltpu.CompilerParams(
            dimension_semantics=("parallel",), vmem_limit_bytes=VMEM_LIMIT),
        name="mem_kv",
    )(mem, g, w_ckv)


def _tail_kernel(x_ref, a_ref, b_ref, kt_ref, v_ref, wout_ref, gc_ref, wcq_ref, wco_ref,
                 gf_ref, w1_ref, w2_ref, gfin_ref, o_ref, *, ff_chunk):
    d = x_ref.shape[-1]
    hd = d // X_HEADS
    merged = jnp.concatenate([a_ref[...], b_ref[...]], axis=1)
    h = x_ref[...] + _dot(merged, wout_ref[...])

    hn = _rms(h, gc_ref[...]).astype(BF16)
    qx = (_dot(hn, wcq_ref[...]) * (1.0 / math.sqrt(hd))).astype(BF16)
    heads = []
    for i in range(X_HEADS):
        cols = slice(i * hd, (i + 1) * hd)
        sc = _dot(qx[:, cols], kt_ref[cols, :])
        e = jnp.exp(sc - jnp.max(sc, axis=-1, keepdims=True))
        p = e / jnp.sum(e, axis=-1, keepdims=True)
        heads.append(_dot(p.astype(BF16), v_ref[:, cols]).astype(BF16))
    h = h + _dot(jnp.concatenate(heads, axis=1), wco_ref[...])

    hn = _rms(h, gf_ref[...]).astype(BF16)
    d_ff = w1_ref.shape[1]
    ff = None
    for c in range(d_ff // ff_chunk):
        cols = slice(c * ff_chunk, (c + 1) * ff_chunk)
        r = jnp.maximum(_dot(hn, w1_ref[:, cols]), 0.0)
        part = _dot((r * r).astype(BF16), w2_ref[cols, :])
        ff = part if ff is None else ff + part
    o_ref[...] = _rms(h + ff, gfin_ref[...])


def _tail(x, a, bo, kt, vm, w_out, gc, w_cq, w_co, gf, w1, w2, gfin, *, tm=512, ff_chunk=1024):
    b, s, d = x.shape
    m = vm.shape[1]
    tok = lambda width: pl.BlockSpec((None, tm, width), lambda i, j: (i, j, 0))
    return pl.pallas_call(
        functools.partial(_tail_kernel, ff_chunk=ff_chunk),
        grid=(b, s // tm),
        in_specs=[tok(d), tok(GM_WIDTH), tok(SB_WIDTH),
                  pl.BlockSpec((None, d, m), lambda i, j: (i, 0, 0)),
                  pl.BlockSpec((None, m, d), lambda i, j: (i, 0, 0)),
                  _resident(w_out.shape), _resident(gc.shape), _resident(w_cq.shape),
                  _resident(w_co.shape), _resident(gf.shape), _resident(w1.shape),
                  _resident(w2.shape), _resident(gfin.shape)],
        out_specs=tok(d),
        out_shape=jax.ShapeDtypeStruct((b, s, d), x.dtype),
        compiler_params=pltpu.CompilerParams(
            dimension_semantics=("parallel", "parallel"), vmem_limit_bytes=VMEM_LIMIT),
        name="tail",
    )(x, a, bo, kt, vm, w_out, gc, w_cq, w_co, gf, w1, w2, gfin)


def _suffix_sum_matrix():
    s = lax.broadcasted_iota(jnp.int32, (SB_BLOCK, SB_BLOCK), 0)
    j = lax.broadcasted_iota(jnp.int32, (SB_BLOCK, SB_BLOCK), 1)
    upper = (j > s).astype(BF16)
    return jnp.concatenate([upper, upper], axis=1)


def kernel(x, mem, norm_mix_g, w_in, gm_v_norm_g, w_spatial, b_spatial, head_norm_g, w_out,
           norm_cross_g, norm_mem_g, w_cq, w_ckv, w_co, norm_ffn_g, w_ff1, w_ff2, norm_final_g):
    assert w_in.shape[0] == 1, "single trunk layer only"
    tril = jnp.tril(jnp.ones((CHUNK, CHUNK), dtype=bool))
    row = lambda g: g.reshape(1, -1).astype(F32)
    wsp = jnp.where(tril[None], w_spatial[0], 0.0).astype(BF16)
    bsp = jnp.broadcast_to(b_spatial[0][:, :, None], (GM_GROUPS, CHUNK, GM_DIM)).astype(F32)
    hg = row(head_norm_g[0])
    a, q, k, v = _mix_in(x, row(norm_mix_g[0]), w_in[0].astype(BF16), row(gm_v_norm_g[0]),
                         wsp, bsp, hg[:, :GM_WIDTH])
    sb = _sb_attn(q, k, v, _suffix_sum_matrix(), hg[:, GM_WIDTH:])
    kt, vm = _mem_kv(mem, row(norm_mem_g[0]), w_ckv[0].astype(BF16))
    return _tail(x, a, sb, kt, vm, w_out[0].astype(BF16), row(norm_cross_g[0]),
                 w_cq[0].astype(BF16), w_co[0].astype(BF16), row(norm_ffn_g[0]),
                 w_ff1[0].astype(BF16), w_ff2[0].astype(BF16), row(norm_final_g))
```

```python
import functools
import math

import jax
import jax.numpy as jnp
from jax import lax
from jax.experimental import pallas as pl
from jax.experimental.pallas import tpu as pltpu

EPS = 1e-6
CHUNK = 128
GM_GROUPS = 4
GM_DIM = 128
GM_WIDTH = GM_GROUPS * GM_DIM
SB_HEADS = 8
SB_HEAD_DIM = 64
SB_WIDTH = SB_HEADS * SB_HEAD_DIM
SB_BLOCK = 128
SB_QBLOCKS = 2
X_HEADS = 4

LANES = 128
VMEM_LIMIT = 56 * 1024 * 1024

LOG2_ZERO = -151.0
LOG2_E = 1.4426950408889634

F32 = jnp.float32
BF16 = jnp.bfloat16


def _dot(a, b):
    return jnp.dot(a, b, preferred_element_type=F32)


def _rms(x, g):
    return x * lax.rsqrt(jnp.mean(x * x, axis=-1, keepdims=True) + EPS) * g


def _resident(shape):
    zeros = (0,) * len(shape)
    return pl.BlockSpec(shape, lambda *_: zeros, pipeline_mode=pl.Buffered(1))


def _mix_in_kernel(x_ref, g_ref, win_ref, gvg_ref, wsp_ref, bsp_ref, hg_ref,
                   a_ref, q_ref, k_ref, vt_ref, *, tm):
    xn = _rms(x_ref[...], g_ref[...]).astype(BF16)
    u = jax.nn.gelu(_dot(xn, win_ref[:, 0:GM_WIDTH]))
    gv = jax.nn.gelu(_dot(xn, win_ref[:, GM_WIDTH:2 * GM_WIDTH]))
    for g in range(GM_GROUPS):
        cols = slice(g * GM_DIM, (g + 1) * GM_DIM)
        gvn = _rms(gv[:, cols], gvg_ref[:, cols]).astype(BF16)
        for c in range(tm // CHUNK):
            rows = slice(c * CHUNK, (c + 1) * CHUNK)
            mixed = _dot(wsp_ref[g], gvn[rows]) + bsp_ref[g]
            gated = u[rows, cols] * mixed
            a_ref[rows, cols] = _rms(gated, hg_ref[:, cols]).astype(BF16)
    base = 2 * GM_WIDTH
    scale = LOG2_E / math.sqrt(SB_HEAD_DIM)
    q_ref[...] = (_dot(xn, win_ref[:, base:base + SB_WIDTH]) * scale).astype(BF16)
    k_ref[...] = _dot(xn, win_ref[:, base + SB_WIDTH:base + 2 * SB_WIDTH]).astype(BF16)
    v = _dot(xn, win_ref[:, base + 2 * SB_WIDTH:base + 3 * SB_WIDTH])
    for c in range(tm // SB_BLOCK):
        vt_ref[c] = v[c * SB_BLOCK:(c + 1) * SB_BLOCK, :].T.astype(BF16)


def _mix_in(x, g, w_in, gvg, wsp, bsp, hg, *, tm=512):
    b, s, d = x.shape
    assert s % tm == 0
    tok = lambda width: pl.BlockSpec((None, tm, width), lambda i, j: (i, j, 0))
    out = jax.ShapeDtypeStruct((b, s, GM_WIDTH), BF16)
    vt_out = jax.ShapeDtypeStruct((b, s // SB_BLOCK, SB_WIDTH, SB_BLOCK), BF16)
    vt_spec = pl.BlockSpec((None, tm // SB_BLOCK, SB_WIDTH, SB_BLOCK), lambda i, j: (i, j, 0, 0))
    return pl.pallas_call(
        functools.partial(_mix_in_kernel, tm=tm),
        grid=(b, s // tm),
        in_specs=[tok(d), _resident(g.shape), _resident(w_in.shape), _resident(gvg.shape),
                  _resident(wsp.shape), _resident(bsp.shape), _resident(hg.shape)],
        out_specs=[tok(GM_WIDTH)] * 3 + [vt_spec],
        out_shape=[out] * 3 + [vt_out],
        compiler_params=pltpu.CompilerParams(
            dimension_semantics=("parallel", "parallel"), vmem_limit_bytes=VMEM_LIMIT),
        name="mix_in",
    )(x, g, w_in, gvg, wsp, bsp, hg)


def _mem_kv_kernel(mem_ref, g_ref, w_ref, kt_ref, v_ref):
    d = mem_ref.shape[-1]
    mn = _rms(mem_ref[...], g_ref[...]).astype(BF16)
    kt_ref[...] = _dot(mn, w_ref[:, 0:d]).T.astype(BF16)
    v_ref[...] = _dot(mn, w_ref[:, d:2 * d]).astype(BF16)


def _mem_kv(mem, g, w_ckv):
    b, m, d = mem.shape
    return pl.pallas_call(
        _mem_kv_kernel,
        grid=(b,),
        in_specs=[pl.BlockSpec((None, m, d), lambda i: (i, 0, 0)),
                  _resident(g.shape), _resident(w_ckv.shape)],
        out_specs=[pl.BlockSpec((None, d, m), lambda i: (i, 0, 0)),
                   pl.BlockSpec((None, m, d), lambda i: (i, 0, 0))],
        out_shape=[jax.ShapeDtypeStruct((b, d, m), BF16), jax.ShapeDtypeStruct((b, m, d), BF16)],
        compiler_params=pltpu.CompilerParams(
            dimension_semantics=("parallel",), vmem_limit_bytes=VMEM_LIMIT),
        name="mem_kv",
    )(mem, g, w_ckv)


def _tail_stages(x_ref, a_ref, sb, kt_ref, v_ref, wout_ref, gc_ref, wcq_ref, wco_ref, gf_ref, w1_ref,
                 w2_ref, gfin_ref, o_ref, ff_chunk):
    d = x_ref.shape[-1]
    hd = d // X_HEADS
    d_ff = w1_ref.shape[1]
    st = {}

    def attention():
        h = x_ref[...] + _dot(jnp.concatenate([a_ref[...], sb], axis=1), wout_ref[...])
        hn = _rms(h, gc_ref[...]).astype(BF16)
        qx = (_dot(hn, wcq_ref[...]) * (1.0 / math.sqrt(hd))).astype(BF16)
        heads = []
        for i in range(X_HEADS):
            cols = slice(i * hd, (i + 1) * hd)
            sc = _dot(qx[:, cols], kt_ref[cols, :])
            e = jnp.exp(sc - jnp.max(sc, axis=-1, keepdims=True))
            p = e / jnp.sum(e, axis=-1, keepdims=True)
            heads.append(_dot(p.astype(BF16), v_ref[:, cols]).astype(BF16))
        st["h"] = h + _dot(jnp.concatenate(heads, axis=1), wco_ref[...])
        st["hn"] = _rms(st["h"], gf_ref[...]).astype(BF16)
        st["ff"] = None

    def mlp_chunk(c):
        cols = slice(c * ff_chunk, (c + 1) * ff_chunk)
        r = jnp.maximum(_dot(st["hn"], w1_ref[:, cols]), 0.0)
        part = _dot((r * r).astype(BF16), w2_ref[cols, :])
        st["ff"] = part if st["ff"] is None else st["ff"] + part

    def finish():
        o_ref[...] = _rms(st["h"] + st["ff"], gfin_ref[...])

    return [attention] + [functools.partial(mlp_chunk, c) for c in range(d_ff // ff_chunk)] + [finish]


def _sb_tail_kernel(q_ref, k_ref, vt_ref, tri_ref, hg_ref,
                    x_ref, a_ref, kt_ref, vm_ref, wout_ref, gc_ref, wcq_ref, wco_ref, gf_ref,
                    w1_ref, w2_ref, gfin_ref, o_ref, qbd_ref, r_ref, acc_ref, sb_ref,
                    *, tiles_per_seq, ff_chunk):
    assert SB_QBLOCKS == 2
    g = pl.program_id(0)
    last = pl.num_programs(0) - 2
    base = (jnp.minimum(g, last) % tiles_per_seq) * SB_QBLOCKS
    slot = g % 2
    pairs = q_ref.shape[-1] // LANES
    cols = [slice(p * LANES, (p + 1) * LANES) for p in range(pairs)]
    top_rows = lax.broadcasted_iota(jnp.int32, (LANES, SB_BLOCK), 0) < SB_HEAD_DIM
    s_idx = lax.broadcasted_iota(jnp.int32, (SB_BLOCK, 2 * SB_BLOCK), 0)
    t_idx = lax.broadcasted_iota(jnp.int32, (SB_BLOCK, 2 * SB_BLOCK), 1) % SB_BLOCK
    strictly_causal = s_idx < t_idx

    @pl.when(g == 0)
    def _():
        sb_ref[...] = jnp.zeros_like(sb_ref)

    def block_diag_heads(x):
        zero = jnp.zeros_like(x)
        return jnp.concatenate([jnp.where(top_rows, x, zero), jnp.where(top_rows, zero, x)], axis=1)

    for c in range(SB_QBLOCKS):
        q_t = q_ref[c * SB_BLOCK:(c + 1) * SB_BLOCK, :].astype(F32).T
        for p in range(pairs):
            qbd_ref[c, p] = block_diag_heads(q_t[cols[p], :].astype(BF16))

    def scores(blocks):
        chains = [(c, j, ok, p) for c, j, ok in blocks for p in range(pairs)]
        z = [_dot(k_ref[pl.ds(pl.multiple_of(j * SB_BLOCK, SB_BLOCK), SB_BLOCK), cols[p]], qbd_ref[c, p])
             for c, j, ok, p in chains]
        return chains, z

    def log_weights(z, masked):
        log_beta, first_row, hl = [], [], []
        for zc in z:
            lb = jnp.minimum(zc, 0.0) - jnp.log2(1.0 + jnp.exp2(-jnp.abs(zc)))
            lm = lb - zc
            if masked:
                lm = jnp.where(strictly_causal, lm, 0.0)
            hi = lm.astype(BF16)
            lo = (lm - hi.astype(F32)).astype(BF16)
            log_beta.append(lb)
            first_row.append(lm[0:1, :])
            hl.append(jnp.concatenate([hi, lo], axis=0))
        suffix = [_dot(tri_ref[...], x) for x in hl]
        total = [sf[0:1, :] + fr for sf, fr in zip(suffix, first_row)]
        return log_beta, suffix, total

    def weighted_values(chains, log_beta, suffix, total, r, masked):
        a_t = []
        for i, (c, j, ok, p) in enumerate(chains):
            r_in = r[c, p] if ok is True else jnp.where(ok, r[c, p], -jnp.inf)
            a = jnp.exp2(log_beta[i] + suffix[i] + r_in)
            if masked:
                a = jnp.where(strictly_causal, a, 0.0)
            a = a.astype(BF16)
            a_t.append(jnp.concatenate([a[:, :SB_BLOCK], a[:, SB_BLOCK:]], axis=0))
            r[c, p] = r[c, p] + (total[i] if ok is True else jnp.where(ok, total[i], 0.0))
        return [_dot(block_diag_heads(vt_ref[j, cols[p], :]), a_t[i])
                for i, (c, j, ok, p) in enumerate(chains)]

    def sweep(waves, fresh=False, fillers=()):
        r = {}
        for blocks, _ in waves:
            for c, _, _ in blocks:
                for p in range(pairs):
                    if (c, p) not in r:
                        r[c, p] = jnp.zeros((1, 2 * SB_BLOCK), F32) if fresh else r_ref[c, p]
        out = {}
        staged = []
        for w in range(len(waves) + 2):
            if w < len(waves):
                blocks, masked = waves[w]
                chains, z = scores(blocks)
                staged.append([chains, z, masked])
            if 1 <= w <= len(waves):
                item = staged[w - 1]
                item[1:2] = [log_weights(item[1], item[2])]
            if 2 <= w:
                chains, (log_beta, suffix, total), masked = staged[w - 2]
                pv = weighted_values(chains, log_beta, suffix, total, r, masked)
                for (c, j, ok, p), x in zip(chains, pv):
                    out[c, p] = x if (c, p) not in out else out[c, p] + x
            for filler in fillers[w:w + 1]:
                filler()
        for filler in fillers[len(waves) + 2:]:
            filler()
        bound = None
        for (c, p), x in out.items():
            acc_ref[c, p] = x if fresh else acc_ref[c, p] + x
            r_ref[c, p] = r[c, p]
            bound = r[c, p] if bound is None else jnp.maximum(bound, r[c, p])
        return jnp.max(bound)

    every = range(SB_QBLOCKS)

    def left(m):
        return ([(c, jnp.maximum(base + c - m, 0), base + c - m >= 0) for c in every], False)

    tail = _tail_stages(x_ref, a_ref, sb_ref[1 - slot], kt_ref, vm_ref, wout_ref, gc_ref, wcq_ref,
                        wco_ref, gf_ref, w1_ref, w2_ref, gfin_ref, o_ref, ff_chunk)
    bound = sweep([([(c, base + c, True) for c in every], True), left(1), left(2)], fresh=True,
                  fillers=tail)

    def cond(carry):
        m, bound = carry
        return jnp.logical_and(m <= base, bound >= LOG2_ZERO)

    def body(carry):
        m, _ = carry
        return m + 1, sweep([([(c, base + c - m, True) for c in every], False)])

    m, bound = lax.while_loop(cond, body, (jnp.int32(3), bound))

    @pl.when(jnp.logical_and(bound >= LOG2_ZERO, m == base + 1))
    def _():
        sweep([([(1, 0, True)], False)])

    for c in every:
        for p in range(pairs):
            acc = acc_ref[c, p]
            sq = acc * acc
            ms0 = jnp.sum(sq[:SB_HEAD_DIM], axis=0, keepdims=True)
            ms1 = jnp.sum(sq[SB_HEAD_DIM:], axis=0, keepdims=True)
            ms = jnp.where(top_rows, ms0, ms1) * (1.0 / SB_HEAD_DIM)
            normed = (acc * lax.rsqrt(ms + EPS)).T
            sb_ref[slot, c * SB_BLOCK:(c + 1) * SB_BLOCK, cols[p]] = (normed * hg_ref[:, cols[p]]).astype(BF16)


def _sb_tail(q, k, vt, tri, hg, x, a, kt, vm, w_out, gc, w_cq, w_co, gf, w1, w2, gfin, *, ff_chunk=1024):
    b, s, d = x.shape
    width = q.shape[-1]
    pairs = width // LANES
    m = vm.shape[1]
    tm = SB_QBLOCKS * SB_BLOCK
    assert s % tm == 0
    nt = s // tm
    n = b * nt
    att = lambda g: jnp.minimum(g, n - 1)
    fin = lambda g: jnp.maximum(g - 1, 0)
    tile = lambda w, t: pl.BlockSpec((None, tm, w), lambda g: (t(g) // nt, t(g) % nt, 0))
    per_seq = lambda shape, t: pl.BlockSpec((None,) + shape, lambda g: (t(g) // nt,) + (0,) * len(shape))
    return pl.pallas_call(
        functools.partial(_sb_tail_kernel, tiles_per_seq=nt, ff_chunk=ff_chunk),
        grid=(n + 1,),
        in_specs=[tile(width, att), per_seq((s, width), att), per_seq(vt.shape[1:], att),
                  _resident(tri.shape), _resident(hg.shape),
                  tile(d, fin), tile(GM_WIDTH, fin), per_seq((d, m), fin), per_seq((m, d), fin),
                  _resident(w_out.shape), _resident(gc.shape), _resident(w_cq.shape),
                  _resident(w_co.shape), _resident(gf.shape), _resident(w1.shape),
                  _resident(w2.shape), _resident(gfin.shape)],
        out_specs=tile(d, fin),
        out_shape=jax.ShapeDtypeStruct((b, s, d), x.dtype),
        scratch_shapes=[pltpu.VMEM((SB_QBLOCKS, pairs, LANES, 2 * SB_BLOCK), BF16),
                        pltpu.VMEM((SB_QBLOCKS, pairs, 1, 2 * SB_BLOCK), F32),
                        pltpu.VMEM((SB_QBLOCKS, pairs, LANES, SB_BLOCK), F32),
                        pltpu.VMEM((2, tm, width), BF16)],
        compiler_params=pltpu.CompilerParams(
            dimension_semantics=("arbitrary",), vmem_limit_bytes=VMEM_LIMIT),
        name="sb_tail",
    )(q, k, vt, tri, hg, x, a, kt, vm, w_out, gc, w_cq, w_co, gf, w1, w2, gfin)


def _suffix_sum_matrix():
    s = lax.broadcasted_iota(jnp.int32, (SB_BLOCK, SB_BLOCK), 0)
    j = lax.broadcasted_iota(jnp.int32, (SB_BLOCK, SB_BLOCK), 1)
    upper = (j > s).astype(BF16)
    return jnp.concatenate([upper, upper], axis=1)


def kernel(x, mem, norm_mix_g, w_in, gm_v_norm_g, w_spatial, b_spatial, head_norm_g, w_out,
           norm_cross_g, norm_mem_g, w_cq, w_ckv, w_co, norm_ffn_g, w_ff1, w_ff2, norm_final_g):
    assert w_in.shape[0] == 1, "single trunk layer only"
    tril = jnp.tril(jnp.ones((CHUNK, CHUNK), dtype=bool))
    row = lambda g: g.reshape(1, -1).astype(F32)
    wsp = jnp.where(tril[None], w_spatial[0], 0.0).astype(BF16)
    bsp = jnp.broadcast_to(b_spatial[0][:, :, None], (GM_GROUPS, CHUNK, GM_DIM)).astype(F32)
    hg = row(head_norm_g[0])
    a, q, k, vt = _mix_in(x, row(norm_mix_g[0]), w_in[0].astype(BF16), row(gm_v_norm_g[0]),
                          wsp, bsp, hg[:, :GM_WIDTH])
    kt, vm = _mem_kv(mem, row(norm_mem_g[0]), w_ckv[0].astype(BF16))
    return _sb_tail(q, k, vt, _suffix_sum_matrix(), hg[:, GM_WIDTH:], x, a, kt, vm,
                    w_out[0].astype(BF16), row(norm_cross_g[0]), w_cq[0].astype(BF16),
                    w_co[0].astype(BF16), row(norm_ffn_g[0]), w_ff1[0].astype(BF16),
                    w_ff2[0].astype(BF16), row(norm_final_g))
```

```python
import functools
import math

import jax
import jax.numpy as jnp
from jax import lax
from jax.experimental import pallas as pl
from jax.experimental.pallas import tpu as pltpu

EPS = 1e-6
CHUNK = 128
GM_GROUPS = 4
GM_DIM = 128
GM_WIDTH = GM_GROUPS * GM_DIM
SB_HEADS = 8
SB_HEAD_DIM = 64
SB_WIDTH = SB_HEADS * SB_HEAD_DIM
SB_BLOCK = 128
SB_QBLOCKS = 2
X_HEADS = 4

LANES = 128
VMEM_LIMIT = 56 * 1024 * 1024

LOG2_ZERO = -151.0
LOG2_E = 1.4426950408889634

F32 = jnp.float32
BF16 = jnp.bfloat16


def _dot(a, b):
    return jnp.dot(a, b, preferred_element_type=F32)


def _rms(x, g):
    return x * lax.rsqrt(jnp.mean(x * x, axis=-1, keepdims=True) + EPS) * g


def _resident(shape):
    zeros = (0,) * len(shape)
    return pl.BlockSpec(shape, lambda *_: zeros, pipeline_mode=pl.Buffered(1))


def _mix_in_kernel(x_ref, g_ref, win_ref, gvg_ref, wsp_ref, bsp_ref, hg_ref,
                   a_ref, q_ref, k_ref, vt_ref, *, tm):
    xn = _rms(x_ref[...], g_ref[...]).astype(BF16)
    base = 2 * GM_WIDTH
    scale = LOG2_E / math.sqrt(SB_HEAD_DIM)
    u_raw = _dot(xn, win_ref[:, 0:GM_WIDTH])
    gv_raw = _dot(xn, win_ref[:, GM_WIDTH:2 * GM_WIDTH])
    gv = jax.nn.gelu(gv_raw)
    q_ref[...] = (_dot(xn, win_ref[:, base:base + SB_WIDTH]) * scale).astype(BF16)
    u = jax.nn.gelu(u_raw)
    k_ref[...] = _dot(xn, win_ref[:, base + SB_WIDTH:base + 2 * SB_WIDTH]).astype(BF16)
    v = _dot(xn, win_ref[:, base + 2 * SB_WIDTH:base + 3 * SB_WIDTH])
    for g in range(GM_GROUPS):
        cols = slice(g * GM_DIM, (g + 1) * GM_DIM)
        gvn = _rms(gv[:, cols], gvg_ref[:, cols]).astype(BF16)
        for c in range(tm // CHUNK):
            rows = slice(c * CHUNK, (c + 1) * CHUNK)
            mixed = _dot(wsp_ref[g], gvn[rows]) + bsp_ref[g]
            gated = u[rows, cols] * mixed
            a_ref[rows, cols] = _rms(gated, hg_ref[:, cols]).astype(BF16)
    for c in range(tm // SB_BLOCK):
        vt_ref[c] = v[c * SB_BLOCK:(c + 1) * SB_BLOCK, :].T.astype(BF16)


def _mix_in(x, g, w_in, gvg, wsp, bsp, hg, *, tm=1024):
    b, s, d = x.shape
    assert s % tm == 0
    tok = lambda width: pl.BlockSpec((None, tm, width), lambda i, j: (i, j, 0))
    out = jax.ShapeDtypeStruct((b, s, GM_WIDTH), BF16)
    vt_out = jax.ShapeDtypeStruct((b, s // SB_BLOCK, SB_WIDTH, SB_BLOCK), BF16)
    vt_spec = pl.BlockSpec((None, tm // SB_BLOCK, SB_WIDTH, SB_BLOCK), lambda i, j: (i, j, 0, 0))
    return pl.pallas_call(
        functools.partial(_mix_in_kernel, tm=tm),
        grid=(b, s // tm),
        in_specs=[tok(d), _resident(g.shape), _resident(w_in.shape), _resident(gvg.shape),
                  _resident(wsp.shape), _resident(bsp.shape), _resident(hg.shape)],
        out_specs=[tok(GM_WIDTH)] * 3 + [vt_spec],
        out_shape=[out] * 3 + [vt_out],
        compiler_params=pltpu.CompilerParams(
            dimension_semantics=("parallel", "parallel"), vmem_limit_bytes=VMEM_LIMIT),
        name="mix_in",
    )(x, g, w_in, gvg, wsp, bsp, hg)


def _mem_kv_kernel(mem_ref, g_ref, w_ref, kt_ref, v_ref):
    d = mem_ref.shape[-1]
    mn = _rms(mem_ref[...], g_ref[...]).astype(BF16)
    kt_ref[...] = _dot(mn, w_ref[:, 0:d]).T.astype(BF16)
    v_ref[...] = _dot(mn, w_ref[:, d:2 * d]).astype(BF16)


def _mem_kv(mem, g, w_ckv):
    b, m, d = mem.shape
    return pl.pallas_call(
        _mem_kv_kernel,
        grid=(b,),
        in_specs=[pl.BlockSpec((None, m, d), lambda i: (i, 0, 0)),
                  _resident(g.shape), _resident(w_ckv.shape)],
        out_specs=[pl.BlockSpec((None, d, m), lambda i: (i, 0, 0)),
                   pl.BlockSpec((None, m, d), lambda i: (i, 0, 0))],
        out_shape=[jax.ShapeDtypeStruct((b, d, m), BF16), jax.ShapeDtypeStruct((b, m, d), BF16)],
        compiler_params=pltpu.CompilerParams(
            dimension_semantics=("parallel",), vmem_limit_bytes=VMEM_LIMIT),
        name="mem_kv",
    )(mem, g, w_ckv)


def _tail_stages(x_ref, a_ref, sb, kt_ref, v_ref, wout_ref, gc_ref, wcq_ref, wco_ref, gf_ref, w1_ref,
                 w2_ref, gfin_ref, o_ref, ff_chunk):
    d = x_ref.shape[-1]
    hd = d // X_HEADS
    d_ff = w1_ref.shape[1]
    st = {}

    def attention():
        h = x_ref[...] + _dot(jnp.concatenate([a_ref[...], sb], axis=1), wout_ref[...])
        hn = _rms(h, gc_ref[...]).astype(BF16)
        qx = (_dot(hn, wcq_ref[...]) * (1.0 / math.sqrt(hd))).astype(BF16)
        heads = []
        for i in range(X_HEADS):
            cols = slice(i * hd, (i + 1) * hd)
            sc = _dot(qx[:, cols], kt_ref[cols, :])
            e = jnp.exp(sc - jnp.max(sc, axis=-1, keepdims=True))
            p = e / jnp.sum(e, axis=-1, keepdims=True)
            heads.append(_dot(p.astype(BF16), v_ref[:, cols]).astype(BF16))
        st["h"] = h + _dot(jnp.concatenate(heads, axis=1), wco_ref[...])
        st["hn"] = _rms(st["h"], gf_ref[...]).astype(BF16)
        st["ff"] = None

    def mlp_chunk(c):
        cols = slice(c * ff_chunk, (c + 1) * ff_chunk)
        r = jnp.maximum(_dot(st["hn"], w1_ref[:, cols]), 0.0)
        part = _dot((r * r).astype(BF16), w2_ref[cols, :])
        st["ff"] = part if st["ff"] is None else st["ff"] + part

    def finish():
        o_ref[...] = _rms(st["h"] + st["ff"], gfin_ref[...])

    return [attention] + [functools.partial(mlp_chunk, c) for c in range(d_ff // ff_chunk)] + [finish]


def _sb_tail_kernel(q_ref, k_ref, vt_ref, tri_ref, hg_ref,
                    x_ref, a_ref, kt_ref, vm_ref, wout_ref, gc_ref, wcq_ref, wco_ref, gf_ref,
                    w1_ref, w2_ref, gfin_ref, o_ref, qbd_ref, r_ref, acc_ref,
                    *, tiles_per_seq, ff_chunk):
    g = pl.program_id(0)
    last = pl.num_programs(0) - 2
    base = (jnp.minimum(g, last) % tiles_per_seq) * SB_QBLOCKS
    slot = g % 2
    pairs = q_ref.shape[-1] // LANES
    cols = [slice(p * LANES, (p + 1) * LANES) for p in range(pairs)]
    top_rows = lax.broadcasted_iota(jnp.int32, (LANES, SB_BLOCK), 0) < SB_HEAD_DIM
    s_idx = lax.broadcasted_iota(jnp.int32, (SB_BLOCK, 2 * SB_BLOCK), 0)
    t_idx = lax.broadcasted_iota(jnp.int32, (SB_BLOCK, 2 * SB_BLOCK), 1) % SB_BLOCK
    strictly_causal = s_idx < t_idx

    @pl.when(g == 0)
    def _():
        acc_ref[...] = jnp.zeros_like(acc_ref)

    def block_diag_heads(x):
        zero = jnp.zeros_like(x)
        return jnp.concatenate([jnp.where(top_rows, x, zero), jnp.where(top_rows, zero, x)], axis=1)

    for c in range(SB_QBLOCKS):
        q_t = q_ref[c * SB_BLOCK:(c + 1) * SB_BLOCK, :].astype(F32).T
        for p in range(pairs):
            qbd_ref[c, p] = block_diag_heads(q_t[cols[p], :].astype(BF16))

    def scores(blocks):
        chains = [(c, j, ok, p) for c, j, ok in blocks for p in range(pairs)]
        z = [_dot(k_ref[pl.ds(pl.multiple_of(j * SB_BLOCK, SB_BLOCK), SB_BLOCK), cols[p]], qbd_ref[c, p])
             for c, j, ok, p in chains]
        return chains, z

    def log_weights(z, masked):
        log_beta, first_row, hl = [], [], []
        for zc in z:
            lb = jnp.minimum(zc, 0.0) - jnp.log2(1.0 + jnp.exp2(-jnp.abs(zc)))
            lm = lb - zc
            if masked:
                lm = jnp.where(strictly_causal, lm, 0.0)
            hi = lm.astype(BF16)
            lo = (lm - hi.astype(F32)).astype(BF16)
            log_beta.append(lb)
            first_row.append(lm[0:1, :])
            hl.append(jnp.concatenate([hi, lo], axis=0))
        suffix = [_dot(tri_ref[...], x) for x in hl]
        total = [sf[0:1, :] + fr for sf, fr in zip(suffix, first_row)]
        return log_beta, suffix, total

    def weights(chains, log_beta, suffix, total, r, masked):
        a_t = []
        for i, (c, j, ok, p) in enumerate(chains):
            r_in = r[c, p] if ok is True else jnp.where(ok, r[c, p], -jnp.inf)
            a = jnp.exp2(log_beta[i] + suffix[i] + r_in)
            if masked:
                a = jnp.where(strictly_causal, a, 0.0)
            a = a.astype(BF16)
            a_t.append(jnp.concatenate([a[:, :SB_BLOCK], a[:, SB_BLOCK:]], axis=0))
            r[c, p] = r[c, p] + (total[i] if ok is True else jnp.where(ok, total[i], 0.0))
        return a_t

    def values(j, p, a_ts):
        v_bd = block_diag_heads(vt_ref[j, cols[p], :])
        pv = _dot(v_bd, a_ts[0] if len(a_ts) == 1 else jnp.concatenate(a_ts, axis=1))
        return [pv[:, i * SB_BLOCK:(i + 1) * SB_BLOCK] for i in range(len(a_ts))]

    def sweep(waves, fresh=False, fillers=()):
        r = {}
        for blocks, _ in waves:
            for c, _, _ in blocks:
                for p in range(pairs):
                    if (c, p) not in r:
                        r[c, p] = jnp.zeros((1, 2 * SB_BLOCK), F32) if fresh else r_ref[slot, c, p]
        out = {}

        def add(c, p, x):
            out[c, p] = x if (c, p) not in out else out[c, p] + x

        staged, waiting = [], {}
        for w in range(len(waves) + 3):
            if w < len(waves):
                blocks, masked = waves[w]
                chains, z = scores(blocks)
                staged.append([chains, z, masked])
            if 1 <= w <= len(waves):
                item = staged[w - 1]
                item[1:2] = [log_weights(item[1], item[2])]
            if 2 <= w:
                done = w - 2
                if done < len(waves):
                    chains, (log_beta, suffix, total), masked = staged[done]
                    a_t = weights(chains, log_beta, suffix, total, r, masked)
                else:
                    chains, a_t = [], []
                for (c, j, ok, p), a in zip(chains, a_t):
                    partner = waiting.pop((c - 1, done - 1, p), None)
                    if partner is not None:
                        first, second = values(j, p, [partner[1], a])
                        add(c - 1, p, first)
                        add(c, p, second)
                    elif c + 1 < SB_QBLOCKS and done + 1 < len(waves):
                        waiting[c, done, p] = (j, a)
                    else:
                        add(c, p, values(j, p, [a])[0])
                for (c, d, p) in [key for key in waiting if key[1] < done]:
                    j, a = waiting.pop((c, d, p))
                    add(c, p, values(j, p, [a])[0])
            if w < len(waves) + 2:
                for filler in fillers[w:w + 1]:
                    filler()
        bound = None
        for (c, p), x in out.items():
            acc_ref[slot, c, p] = x if fresh else acc_ref[slot, c, p] + x
            r_ref[slot, c, p] = r[c, p]
            bound = r[c, p] if bound is None else jnp.maximum(bound, r[c, p])
        bound = jnp.max(bound)
        for filler in fillers[len(waves) + 2:]:
            filler()
        return bound

    every = range(SB_QBLOCKS)

    def left(m):
        return ([(c, jnp.maximum(base + c - m, 0), base + c - m >= 0) for c in every], False)

    def finished_attention(acc_slot):
        rows = []
        for c in every:
            row = []
            for p in range(pairs):
                acc = acc_ref[acc_slot, c, p]
                sq = acc * acc
                ms0 = jnp.sum(sq[:SB_HEAD_DIM], axis=0, keepdims=True)
                ms1 = jnp.sum(sq[SB_HEAD_DIM:], axis=0, keepdims=True)
                ms = jnp.where(top_rows, ms0, ms1) * (1.0 / SB_HEAD_DIM)
                normed = (acc * lax.rsqrt(ms + EPS)).T
                row.append((normed * hg_ref[:, cols[p]]).astype(BF16))
            rows.append(jnp.concatenate(row, axis=1))
        return jnp.concatenate(rows, axis=0)

    tail = _tail_stages(x_ref, a_ref, finished_attention(1 - slot), kt_ref, vm_ref, wout_ref, gc_ref,
                        wcq_ref, wco_ref, gf_ref, w1_ref, w2_ref, gfin_ref, o_ref, ff_chunk)
    bound = sweep([([(c, base + c, True) for c in every], True), left(1), left(2)], fresh=True,
                  fillers=tail)

    def cond(carry):
        m, bound = carry
        return jnp.logical_and(m < base + SB_QBLOCKS, bound >= LOG2_ZERO)

    def body(carry):
        m, _ = carry
        return m + 1, sweep([left(m)])

    lax.while_loop(cond, body, (jnp.int32(3), bound))


def _sb_tail(q, k, vt, tri, hg, x, a, kt, vm, w_out, gc, w_cq, w_co, gf, w1, w2, gfin, *, ff_chunk=1024):
    b, s, d = x.shape
    width = q.shape[-1]
    pairs = width // LANES
    m = vm.shape[1]
    tm = SB_QBLOCKS * SB_BLOCK
    assert s % tm == 0
    nt = s // tm
    n = b * nt
    att = lambda g: jnp.minimum(g, n - 1)
    fin = lambda g: jnp.maximum(g - 1, 0)
    tile = lambda w, t: pl.BlockSpec((None, tm, w), lambda g: (t(g) // nt, t(g) % nt, 0))
    per_seq = lambda shape, t: pl.BlockSpec((None,) + shape, lambda g: (t(g) // nt,) + (0,) * len(shape))
    return pl.pallas_call(
        functools.partial(_sb_tail_kernel, tiles_per_seq=nt, ff_chunk=ff_chunk),
        grid=(n + 1,),
        in_specs=[tile(width, att), per_seq((s, width), att), per_seq(vt.shape[1:], att),
                  _resident(tri.shape), _resident(hg.shape),
                  tile(d, fin), tile(GM_WIDTH, fin), per_seq((d, m), fin), per_seq((m, d), fin),
                  _resident(w_out.shape), _resident(gc.shape), _resident(w_cq.shape),
                  _resident(w_co.shape), _resident(gf.shape), _resident(w1.shape),
                  _resident(w2.shape), _resident(gfin.shape)],
        out_specs=tile(d, fin),
        out_shape=jax.ShapeDtypeStruct((b, s, d), x.dtype),
        scratch_shapes=[pltpu.VMEM((SB_QBLOCKS, pairs, LANES, 2 * SB_BLOCK), BF16),
                        pltpu.VMEM((2, SB_QBLOCKS, pairs, 1, 2 * SB_BLOCK), F32),
                        pltpu.VMEM((2, SB_QBLOCKS, pairs, LANES, SB_BLOCK), F32)],
        compiler_params=pltpu.CompilerParams(
            dimension_semantics=("arbitrary",), vmem_limit_bytes=VMEM_LIMIT),
        name="sb_tail",
    )(q, k, vt, tri, hg, x, a, kt, vm, w_out, gc, w_cq, w_co, gf, w1, w2, gfin)


def _suffix_sum_matrix():
    s = lax.broadcasted_iota(jnp.int32, (SB_BLOCK, SB_BLOCK), 0)
    j = lax.broadcasted_iota(jnp.int32, (SB_BLOCK, SB_BLOCK), 1)
    upper = (j > s).astype(BF16)
    return jnp.concatenate([upper, upper], axis=1)


def kernel(x, mem, norm_mix_g, w_in, gm_v_norm_g, w_spatial, b_spatial, head_norm_g, w_out,
           norm_cross_g, norm_mem_g, w_cq, w_ckv, w_co, norm_ffn_g, w_ff1, w_ff2, norm_final_g):
    assert w_in.shape[0] == 1, "single trunk layer only"
    tril = jnp.tril(jnp.ones((CHUNK, CHUNK), dtype=bool))
    row = lambda g: g.reshape(1, -1).astype(F32)
    wsp = jnp.where(tril[None], w_spatial[0], 0.0).astype(BF16)
    bsp = jnp.broadcast_to(b_spatial[0][:, :, None], (GM_GROUPS, CHUNK, GM_DIM)).astype(F32)
    hg = row(head_norm_g[0])
    a, q, k, vt = _mix_in(x, row(norm_mix_g[0]), w_in[0].astype(BF16), row(gm_v_norm_g[0]),
                          wsp, bsp, hg[:, :GM_WIDTH])
    kt, vm = _mem_kv(mem, row(norm_mem_g[0]), w_ckv[0].astype(BF16))
    return _sb_tail(q, k, vt, _suffix_sum_matrix(), hg[:, GM_WIDTH:], x, a, kt, vm,
                    w_out[0].astype(BF16), row(norm_cross_g[0]), w_cq[0].astype(BF16),
                    w_co[0].astype(BF16), row(norm_ffn_g[0]), w_ff1[0].astype(BF16),
                    w_ff2[0].astype(BF16), row(norm_final_g))
```

```python
import functools
import math

import jax
import jax.numpy as jnp
from jax import lax
from jax.experimental import pallas as pl
from jax.experimental.pallas import tpu as pltpu

EPS = 1e-6
CHUNK = 128
GM_GROUPS = 4
GM_DIM = 128
GM_WIDTH = GM_GROUPS * GM_DIM
SB_HEADS = 8
SB_HEAD_DIM = 64
SB_WIDTH = SB_HEADS * SB_HEAD_DIM
SB_BLOCK = 128
SB_QBLOCKS = 4
X_HEADS = 4

LANES = 128
VMEM_LIMIT = 60 * 1024 * 1024

LOG2_ZERO = -151.0
LOG2_E = 1.4426950408889634

F32 = jnp.float32
BF16 = jnp.bfloat16


def _dot(a, b):
    return jnp.dot(a, b, preferred_element_type=F32)


def _inv_rms(x):
    return lax.rsqrt(jnp.mean(x * x, axis=-1, keepdims=True) + EPS)


def _rms(x, g):
    return x * _inv_rms(x) * g


def _resident(shape):
    zeros = (0,) * len(shape)
    return pl.BlockSpec(shape, lambda *_: zeros, pipeline_mode=pl.Buffered(1))


def _mix_in_kernel(x_ref, g_ref, win_ref, gvg_ref, wsp_ref, bsp_ref, hg_ref,
                   a_ref, q_ref, k_ref, vt_ref, *, tm):
    x = x_ref[...]
    xg = (x * g_ref[...]).astype(BF16)
    inv = _inv_rms(x)
    base = 2 * GM_WIDTH
    scale = LOG2_E / math.sqrt(SB_HEAD_DIM)
    u_raw = _dot(xg, win_ref[:, 0:GM_WIDTH]) * inv
    gv_raw = _dot(xg, win_ref[:, GM_WIDTH:2 * GM_WIDTH]) * inv
    gv = jax.nn.gelu(gv_raw)
    q_ref[...] = (_dot(xg, win_ref[:, base:base + SB_WIDTH]) * (inv * scale)).astype(BF16)
    u = jax.nn.gelu(u_raw)
    k_ref[...] = (_dot(xg, win_ref[:, base + SB_WIDTH:base + 2 * SB_WIDTH]) * inv).astype(BF16)
    v = _dot(xg, win_ref[:, base + 2 * SB_WIDTH:base + 3 * SB_WIDTH]) * inv
    for g in range(GM_GROUPS):
        cols = slice(g * GM_DIM, (g + 1) * GM_DIM)
        gvn = _rms(gv[:, cols], gvg_ref[:, cols]).astype(BF16)
        for c in range(tm // CHUNK):
            rows = slice(c * CHUNK, (c + 1) * CHUNK)
            mixed = _dot(wsp_ref[g], gvn[rows]) + bsp_ref[g]
            gated = u[rows, cols] * mixed
            a_ref[rows, cols] = _rms(gated, hg_ref[:, cols]).astype(BF16)
    for c in range(tm // SB_BLOCK):
        vt_ref[c] = v[c * SB_BLOCK:(c + 1) * SB_BLOCK, :].T.astype(BF16)


def _mix_in(x, g, w_in, gvg, wsp, bsp, hg, *, tm=1024):
    b, s, d = x.shape
    assert s % tm == 0
    tok = lambda width: pl.BlockSpec((None, tm, width), lambda i, j: (i, j, 0))
    out = jax.ShapeDtypeStruct((b, s, GM_WIDTH), BF16)
    vt_out = jax.ShapeDtypeStruct((b, s // SB_BLOCK, SB_WIDTH, SB_BLOCK), BF16)
    vt_spec = pl.BlockSpec((None, tm // SB_BLOCK, SB_WIDTH, SB_BLOCK), lambda i, j: (i, j, 0, 0))
    return pl.pallas_call(
        functools.partial(_mix_in_kernel, tm=tm),
        grid=(b, s // tm),
        in_specs=[tok(d), _resident(g.shape), _resident(w_in.shape), _resident(gvg.shape),
                  _resident(wsp.shape), _resident(bsp.shape), _resident(hg.shape)],
        out_specs=[tok(GM_WIDTH)] * 3 + [vt_spec],
        out_shape=[out] * 3 + [vt_out],
        compiler_params=pltpu.CompilerParams(
            dimension_semantics=("parallel", "parallel"), vmem_limit_bytes=VMEM_LIMIT),
        name="mix_in",
    )(x, g, w_in, gvg, wsp, bsp, hg)


def _mem_kv_kernel(mem_ref, g_ref, w_ref, kt_ref, v_ref):
    d = mem_ref.shape[-1]
    mn = _rms(mem_ref[...], g_ref[...]).astype(BF16)
    kt_ref[...] = _dot(mn, w_ref[:, 0:d]).T.astype(BF16)
    v_ref[...] = _dot(mn, w_ref[:, d:2 * d]).astype(BF16)


def _mem_kv(mem, g, w_ckv):
    b, m, d = mem.shape
    return pl.pallas_call(
        _mem_kv_kernel,
        grid=(b,),
        in_specs=[pl.BlockSpec((None, m, d), lambda i: (i, 0, 0)),
                  _resident(g.shape), _resident(w_ckv.shape)],
        out_specs=[pl.BlockSpec((None, d, m), lambda i: (i, 0, 0)),
                   pl.BlockSpec((None, m, d), lambda i: (i, 0, 0))],
        out_shape=[jax.ShapeDtypeStruct((b, d, m), BF16), jax.ShapeDtypeStruct((b, m, d), BF16)],
        compiler_params=pltpu.CompilerParams(
            dimension_semantics=("parallel",), vmem_limit_bytes=VMEM_LIMIT),
        name="mem_kv",
    )(mem, g, w_ckv)


def _tail_stages(x_ref, a_ref, sb, kt_ref, v_ref, wout_ref, gc_ref, wcq_ref, wco_ref, gf_ref, w1_ref,
                 w2_ref, gfin_ref, o_ref, ff_chunk):
    d = x_ref.shape[-1]
    hd = d // X_HEADS
    d_ff = w1_ref.shape[1]
    st = {}

    def attention():
        h = x_ref[...] + _dot(jnp.concatenate([a_ref[...], sb], axis=1), wout_ref[...])
        qx = _dot((h * gc_ref[...]).astype(BF16), wcq_ref[...])
        qx = (qx * (_inv_rms(h) * (1.0 / math.sqrt(hd)))).astype(BF16)
        heads = []
        for i in range(X_HEADS):
            cols = slice(i * hd, (i + 1) * hd)
            sc = _dot(qx[:, cols], kt_ref[cols, :])
            e = jnp.exp(sc - jnp.max(sc, axis=-1, keepdims=True))
            o = _dot(e.astype(BF16), v_ref[:, cols]) / jnp.sum(e, axis=-1, keepdims=True)
            heads.append(o.astype(BF16))
        st["h"] = h + _dot(jnp.concatenate(heads, axis=1), wco_ref[...])
        st["hg"] = (st["h"] * gf_ref[...]).astype(BF16)
        st["inv2"] = jnp.square(_inv_rms(st["h"]))
        st["ff"] = None

    def mlp_chunk(c):
        cols = slice(c * ff_chunk, (c + 1) * ff_chunk)
        r = jnp.maximum(_dot(st["hg"], w1_ref[:, cols]), 0.0)
        part = _dot((r * r).astype(BF16), w2_ref[cols, :])
        st["ff"] = part if st["ff"] is None else st["ff"] + part

    def finish():
        o_ref[...] = _rms(st["h"] + st["ff"] * st["inv2"], gfin_ref[...])

    return [attention] + [functools.partial(mlp_chunk, c) for c in range(d_ff // ff_chunk)] + [finish]


def _sb_tail_kernel(q_ref, k_ref, vt_ref, tri_ref, hg_ref,
                    x_ref, a_ref, kt_ref, vm_ref, wout_ref, gc_ref, wcq_ref, wco_ref, gf_ref,
                    w1_ref, w2_ref, gfin_ref, o_ref, qbd_ref, r_ref, acc_ref,
                    *, tiles_per_seq, ff_chunk):
    g = pl.program_id(0)
    last = pl.num_programs(0) - 2
    base = (jnp.minimum(g, last) % tiles_per_seq) * SB_QBLOCKS
    slot = g % 2
    pairs = q_ref.shape[-1] // LANES
    cols = [slice(p * LANES, (p + 1) * LANES) for p in range(pairs)]
    top_rows = lax.broadcasted_iota(jnp.int32, (LANES, SB_BLOCK), 0) < SB_HEAD_DIM
    s_idx = lax.broadcasted_iota(jnp.int32, (SB_BLOCK, 2 * SB_BLOCK), 0)
    t_idx = lax.broadcasted_iota(jnp.int32, (SB_BLOCK, 2 * SB_BLOCK), 1) % SB_BLOCK
    strictly_causal = s_idx < t_idx

    @pl.when(g == 0)
    def _():
        acc_ref[...] = jnp.zeros_like(acc_ref)

    def block_diag_heads(x):
        zero = jnp.zeros_like(x)
        return jnp.concatenate([jnp.where(top_rows, x, zero), jnp.where(top_rows, zero, x)], axis=1)

    for c in range(SB_QBLOCKS):
        q_t = q_ref[c * SB_BLOCK:(c + 1) * SB_BLOCK, :].astype(F32).T
        for p in range(pairs):
            qbd_ref[c, p] = block_diag_heads(q_t[cols[p], :].astype(BF16))

    def scores(blocks):
        chains = [(c, j, ok, p) for c, j, ok in blocks for p in range(pairs)]
        z = [_dot(k_ref[pl.ds(pl.multiple_of(j * SB_BLOCK, SB_BLOCK), SB_BLOCK), cols[p]], qbd_ref[c, p])
             for c, j, ok, p in chains]
        return chains, z

    def log_weights(z, masked):
        log_beta, first_row, hl = [], [], []
        for zc in z:
            lb = jnp.minimum(zc, 0.0) - jnp.log2(1.0 + jnp.exp2(-jnp.abs(zc)))
            lm = lb - zc
            if masked:
                lm = jnp.where(strictly_causal, lm, 0.0)
            hi = lm.astype(BF16)
            lo = (lm - hi.astype(F32)).astype(BF16)
            log_beta.append(lb)
            first_row.append(lm[0:1, :])
            hl.append(jnp.concatenate([hi, lo], axis=0))
        suffix = [_dot(tri_ref[...], x) for x in hl]
        total = [sf[0:1, :] + fr for sf, fr in zip(suffix, first_row)]
        return log_beta, suffix, total

    def weights(chains, log_beta, suffix, total, r, masked):
        a_t = []
        for i, (c, j, ok, p) in enumerate(chains):
            r_in = r[c, p] if ok is True else jnp.where(ok, r[c, p], -jnp.inf)
            a = jnp.exp2(log_beta[i] + suffix[i] + r_in)
            if masked:
                a = jnp.where(strictly_causal, a, 0.0)
            a = a.astype(BF16)
            a_t.append(jnp.concatenate([a[:, :SB_BLOCK], a[:, SB_BLOCK:]], axis=0))
            r[c, p] = r[c, p] + (total[i] if ok is True else jnp.where(ok, total[i], 0.0))
        return a_t

    def values(j, p, a_ts):
        v_bd = block_diag_heads(vt_ref[j, cols[p], :])
        pv = _dot(v_bd, a_ts[0] if len(a_ts) == 1 else jnp.concatenate(a_ts, axis=1))
        return [pv[:, i * SB_BLOCK:(i + 1) * SB_BLOCK] for i in range(len(a_ts))]

    def sweep(waves, fresh=False, fillers=()):
        r = {}
        for blocks, _ in waves:
            for c, _, _ in blocks:
                for p in range(pairs):
                    if (c, p) not in r:
                        r[c, p] = jnp.zeros((1, 2 * SB_BLOCK), F32) if fresh else r_ref[slot, c, p]
        out = {}

        def add(c, p, x):
            out[c, p] = x if (c, p) not in out else out[c, p] + x

        staged, waiting = [], {}
        for w in range(len(waves) + 3):
            if w < len(waves):
                blocks, masked = waves[w]
                chains, z = scores(blocks)
                staged.append([chains, z, masked])
            if 1 <= w <= len(waves):
                item = staged[w - 1]
                item[1:2] = [log_weights(item[1], item[2])]
            if 2 <= w:
                done = w - 2
                if done < len(waves):
                    chains, (log_beta, suffix, total), masked = staged[done]
                    a_t = weights(chains, log_beta, suffix, total, r, masked)
                else:
                    chains, a_t = [], []
                for (c, j, ok, p), a in zip(chains, a_t):
                    partner = waiting.pop((c - 1, done - 1, p), None)
                    if partner is not None:
                        first, second = values(j, p, [partner[1], a])
                        add(c - 1, p, first)
                        add(c, p, second)
                    elif c + 1 < SB_QBLOCKS and done + 1 < len(waves):
                        waiting[c, done, p] = (j, a)
                    else:
                        add(c, p, values(j, p, [a])[0])
                for (c, d, p) in [key for key in waiting if key[1] < done]:
                    j, a = waiting.pop((c, d, p))
                    add(c, p, values(j, p, [a])[0])
            if w < len(waves) + 2:
                for filler in fillers[w:w + 1]:
                    filler()
        bound = None
        for (c, p), x in out.items():
            acc_ref[slot, c, p] = x if fresh else acc_ref[slot, c, p] + x
            r_ref[slot, c, p] = r[c, p]
            bound = r[c, p] if bound is None else jnp.maximum(bound, r[c, p])
        bound = jnp.max(bound)
        for filler in fillers[len(waves) + 2:]:
            filler()
        return bound

    every = range(SB_QBLOCKS)

    def left(m):
        return ([(c, jnp.maximum(base + c - m, 0), base + c - m >= 0) for c in every], False)

    def finished_attention(acc_slot):
        rows = []
        for c in every:
            row = []
            for p in range(pairs):
                acc = acc_ref[acc_slot, c, p]
                sq = acc * acc
                ms0 = jnp.sum(sq[:SB_HEAD_DIM], axis=0, keepdims=True)
                ms1 = jnp.sum(sq[SB_HEAD_DIM:], axis=0, keepdims=True)
                ms = jnp.where(top_rows, ms0, ms1) * (1.0 / SB_HEAD_DIM)
                normed = (acc * lax.rsqrt(ms + EPS)).T
                row.append((normed * hg_ref[:, cols[p]]).astype(BF16))
            rows.append(jnp.concatenate(row, axis=1))
        return jnp.concatenate(rows, axis=0)

    tail = _tail_stages(x_ref, a_ref, finished_attention(1 - slot), kt_ref, vm_ref, wout_ref, gc_ref,
                        wcq_ref, wco_ref, gf_ref, w1_ref, w2_ref, gfin_ref, o_ref, ff_chunk)
    bound = sweep([([(c, base + c, True) for c in every], True), left(1), left(2)], fresh=True,
                  fillers=tail)

    def cond(carry):
        m, bound = carry
        return jnp.logical_and(m < base + SB_QBLOCKS, bound >= LOG2_ZERO)

    def body(carry):
        m, _ = carry
        return m + 1, sweep([left(m)])

    lax.while_loop(cond, body, (jnp.int32(3), bound))


def _sb_tail(q, k, vt, tri, hg, x, a, kt, vm, w_out, gc, w_cq, w_co, gf, w1, w2, gfin, *, ff_chunk=1024):
    b, s, d = x.shape
    width = q.shape[-1]
    pairs = width // LANES
    m = vm.shape[1]
    tm = SB_QBLOCKS * SB_BLOCK
    assert s % tm == 0
    nt = s // tm
    n = b * nt
    att = lambda g: jnp.minimum(g, n - 1)
    fin = lambda g: jnp.maximum(g - 1, 0)
    tile = lambda w, t: pl.BlockSpec((None, tm, w), lambda g: (t(g) // nt, t(g) % nt, 0))
    per_seq = lambda shape, t: pl.BlockSpec((None,) + shape, lambda g: (t(g) // nt,) + (0,) * len(shape),
                                            pipeline_mode=pl.Buffered(1))
    return pl.pallas_call(
        functools.partial(_sb_tail_kernel, tiles_per_seq=nt, ff_chunk=ff_chunk),
        grid=(n + 1,),
        in_specs=[tile(width, att), per_seq((s, width), att), per_seq(vt.shape[1:], att),
                  _resident(tri.shape), _resident(hg.shape),
                  tile(d, fin), tile(GM_WIDTH, fin), per_seq((d, m), fin), per_seq((m, d), fin),
                  _resident(w_out.shape), _resident(gc.shape), _resident(w_cq.shape),
                  _resident(w_co.shape), _resident(gf.shape), _resident(w1.shape),
                  _resident(w2.shape), _resident(gfin.shape)],
        out_specs=tile(d, fin),
        out_shape=jax.ShapeDtypeStruct((b, s, d), x.dtype),
        scratch_shapes=[pltpu.VMEM((SB_QBLOCKS, pairs, LANES, 2 * SB_BLOCK), BF16),
                        pltpu.VMEM((2, SB_QBLOCKS, pairs, 1, 2 * SB_BLOCK), F32),
                        pltpu.VMEM((2, SB_QBLOCKS, pairs, LANES, SB_BLOCK), F32)],
        compiler_params=pltpu.CompilerParams(
            dimension_semantics=("arbitrary",), vmem_limit_bytes=VMEM_LIMIT),
        name="sb_tail",
    )(q, k, vt, tri, hg, x, a, kt, vm, w_out, gc, w_cq, w_co, gf, w1, w2, gfin)


def _suffix_sum_matrix():
    s = lax.broadcasted_iota(jnp.int32, (SB_BLOCK, SB_BLOCK), 0)
    j = lax.broadcasted_iota(jnp.int32, (SB_BLOCK, SB_BLOCK), 1)
    upper = (j > s).astype(BF16)
    return jnp.concatenate([upper, upper], axis=1)


def kernel(x, mem, norm_mix_g, w_in, gm_v_norm_g, w_spatial, b_spatial, head_norm_g, w_out,
           norm_cross_g, norm_mem_g, w_cq, w_ckv, w_co, norm_ffn_g, w_ff1, w_ff2, norm_final_g):
    assert w_in.shape[0] == 1, "single trunk layer only"
    tril = jnp.tril(jnp.ones((CHUNK, CHUNK), dtype=bool))
    row = lambda g: g.reshape(1, -1).astype(F32)
    wsp = jnp.where(tril[None], w_spatial[0], 0.0).astype(BF16)
    bsp = jnp.broadcast_to(b_spatial[0][:, :, None], (GM_GROUPS, CHUNK, GM_DIM)).astype(F32)
    hg = row(head_norm_g[0])
    a, q, k, vt = _mix_in(x, row(norm_mix_g[0]), w_in[0].astype(BF16), row(gm_v_norm_g[0]),
                          wsp, bsp, hg[:, :GM_WIDTH])
    kt, vm = _mem_kv(mem, row(norm_mem_g[0]), w_ckv[0].astype(BF16))
    return _sb_tail(q, k, vt, _suffix_sum_matrix(), hg[:, GM_WIDTH:], x, a, kt, vm,
                    w_out[0].astype(BF16), row(norm_cross_g[0]), w_cq[0].astype(BF16),
                    w_co[0].astype(BF16), row(norm_ffn_g[0]), w_ff1[0].astype(BF16),
                    w_ff2[0].astype(BF16), row(norm_final_g))
```

```python
import functools
import math

import jax
import jax.numpy as jnp
from jax import lax
from jax.experimental import pallas as pl
from jax.experimental.pallas import tpu as pltpu

EPS = 1e-6
CHUNK = 128
GM_GROUPS = 4
GM_DIM = 128
GM_WIDTH = GM_GROUPS * GM_DIM
SB_HEADS = 8
SB_HEAD_DIM = 64
SB_WIDTH = SB_HEADS * SB_HEAD_DIM
SB_BLOCK = 128
SB_QBLOCKS = 2
X_HEADS = 4

LANES = 128
VMEM_LIMIT = 56 * 1024 * 1024

LOG2_ZERO = -151.0
LOG2_E = 1.4426950408889634

F32 = jnp.float32
BF16 = jnp.bfloat16


def _dot(a, b):
    return jnp.dot(a, b, preferred_element_type=F32)


def _inv_rms(x):
    return lax.rsqrt(jnp.mean(x * x, axis=-1, keepdims=True) + EPS)


def _rms(x, g):
    return x * _inv_rms(x) * g


def _resident(shape):
    zeros = (0,) * len(shape)
    return pl.BlockSpec(shape, lambda *_: zeros, pipeline_mode=pl.Buffered(1))


def _mix_in_kernel(x_ref, g_ref, win_ref, gvg_ref, wsp_ref, bsp_ref, hg_ref,
                   a_ref, q_ref, k_ref, vt_ref, *, tm):
    x = x_ref[...]
    xg = (x * g_ref[...]).astype(BF16)
    inv = _inv_rms(x)
    base = 2 * GM_WIDTH
    scale = LOG2_E / math.sqrt(SB_HEAD_DIM)
    u_raw = _dot(xg, win_ref[:, 0:GM_WIDTH]) * inv
    gv_raw = _dot(xg, win_ref[:, GM_WIDTH:2 * GM_WIDTH]) * inv
    gv = jax.nn.gelu(gv_raw)
    q_ref[...] = (_dot(xg, win_ref[:, base:base + SB_WIDTH]) * (inv * scale)).astype(BF16)
    u = jax.nn.gelu(u_raw)
    k_ref[...] = (_dot(xg, win_ref[:, base + SB_WIDTH:base + 2 * SB_WIDTH]) * inv).astype(BF16)
    v = _dot(xg, win_ref[:, base + 2 * SB_WIDTH:base + 3 * SB_WIDTH]) * inv
    for g in range(GM_GROUPS):
        cols = slice(g * GM_DIM, (g + 1) * GM_DIM)
        gvn = _rms(gv[:, cols], gvg_ref[:, cols]).astype(BF16)
        for c in range(tm // CHUNK):
            rows = slice(c * CHUNK, (c + 1) * CHUNK)
            mixed = _dot(wsp_ref[g], gvn[rows]) + bsp_ref[g]
            gated = u[rows, cols] * mixed
            a_ref[rows, cols] = _rms(gated, hg_ref[:, cols]).astype(BF16)
    for c in range(tm // SB_BLOCK):
        vt_ref[c] = v[c * SB_BLOCK:(c + 1) * SB_BLOCK, :].T.astype(BF16)


def _mix_in(x, g, w_in, gvg, wsp, bsp, hg, *, tm=1024):
    b, s, d = x.shape
    assert s % tm == 0
    tok = lambda width: pl.BlockSpec((None, tm, width), lambda i, j: (i, j, 0))
    out = jax.ShapeDtypeStruct((b, s, GM_WIDTH), BF16)
    vt_out = jax.ShapeDtypeStruct((b, s // SB_BLOCK, SB_WIDTH, SB_BLOCK), BF16)
    vt_spec = pl.BlockSpec((None, tm // SB_BLOCK, SB_WIDTH, SB_BLOCK), lambda i, j: (i, j, 0, 0))
    return pl.pallas_call(
        functools.partial(_mix_in_kernel, tm=tm),
        grid=(b, s // tm),
        in_specs=[tok(d), _resident(g.shape), _resident(w_in.shape), _resident(gvg.shape),
                  _resident(wsp.shape), _resident(bsp.shape), _resident(hg.shape)],
        out_specs=[tok(GM_WIDTH)] * 3 + [vt_spec],
        out_shape=[out] * 3 + [vt_out],
        compiler_params=pltpu.CompilerParams(
            dimension_semantics=("parallel", "parallel"), vmem_limit_bytes=VMEM_LIMIT),
        name="mix_in",
    )(x, g, w_in, gvg, wsp, bsp, hg)


def _mem_kv_kernel(mem_ref, g_ref, w_ref, kt_ref, v_ref):
    d = mem_ref.shape[-1]
    mn = _rms(mem_ref[...], g_ref[...]).astype(BF16)
    kt_ref[...] = _dot(mn, w_ref[:, 0:d]).T.astype(BF16)
    v_ref[...] = _dot(mn, w_ref[:, d:2 * d]).astype(BF16)


def _mem_kv(mem, g, w_ckv):
    b, m, d = mem.shape
    return pl.pallas_call(
        _mem_kv_kernel,
        grid=(b,),
        in_specs=[pl.BlockSpec((None, m, d), lambda i: (i, 0, 0)),
                  _resident(g.shape), _resident(w_ckv.shape)],
        out_specs=[pl.BlockSpec((None, d, m), lambda i: (i, 0, 0)),
                   pl.BlockSpec((None, m, d), lambda i: (i, 0, 0))],
        out_shape=[jax.ShapeDtypeStruct((b, d, m), BF16), jax.ShapeDtypeStruct((b, m, d), BF16)],
        compiler_params=pltpu.CompilerParams(
            dimension_semantics=("parallel",), vmem_limit_bytes=VMEM_LIMIT),
        name="mem_kv",
    )(mem, g, w_ckv)


def _tail_stages(x_ref, a_ref, sb, kt_ref, v_ref, wout_ref, gc_ref, wcq_ref, wco_ref, gf_ref, w1_ref,
                 w2_ref, gfin_ref, o_ref, ff_chunk):
    d = x_ref.shape[-1]
    hd = d // X_HEADS
    d_ff = w1_ref.shape[1]
    st = {}

    def attention():
        h = x_ref[...] + _dot(jnp.concatenate([a_ref[...], sb], axis=1), wout_ref[...])
        qx = _dot((h * gc_ref[...]).astype(BF16), wcq_ref[...])
        qx = (qx * (_inv_rms(h) * (1.0 / math.sqrt(hd)))).astype(BF16)
        heads = []
        for i in range(X_HEADS):
            cols = slice(i * hd, (i + 1) * hd)
            sc = _dot(qx[:, cols], kt_ref[cols, :])
            e = jnp.exp(sc - jnp.max(sc, axis=-1, keepdims=True))
            o = _dot(e.astype(BF16), v_ref[:, cols]) / jnp.sum(e, axis=-1, keepdims=True)
            heads.append(o.astype(BF16))
        st["h"] = h + _dot(jnp.concatenate(heads, axis=1), wco_ref[...])
        st["hg"] = (st["h"] * gf_ref[...]).astype(BF16)
        st["inv2"] = jnp.square(_inv_rms(st["h"]))
        st["ff"] = None

    def mlp_chunk(c):
        cols = slice(c * ff_chunk, (c + 1) * ff_chunk)
        r = jnp.maximum(_dot(st["hg"], w1_ref[:, cols]), 0.0)
        part = _dot((r * r).astype(BF16), w2_ref[cols, :])
        st["ff"] = part if st["ff"] is None else st["ff"] + part

    def finish():
        o_ref[...] = _rms(st["h"] + st["ff"] * st["inv2"], gfin_ref[...])

    return [attention] + [functools.partial(mlp_chunk, c) for c in range(d_ff // ff_chunk)] + [finish]


def _sb_tail_kernel(q_ref, k_ref, vt_ref, tri_ref, hg_ref,
                    x_ref, a_ref, kt_ref, vm_ref, wout_ref, gc_ref, wcq_ref, wco_ref, gf_ref,
                    w1_ref, w2_ref, gfin_ref, o_ref, qbd_ref, r_ref, acc_ref,
                    *, tiles_per_seq, ff_chunk):
    g = pl.program_id(0)
    last = pl.num_programs(0) - 2
    base = (jnp.minimum(g, last) % tiles_per_seq) * SB_QBLOCKS
    slot = g % 2
    pairs = q_ref.shape[-1] // LANES
    cols = [slice(p * LANES, (p + 1) * LANES) for p in range(pairs)]
    top_rows = lax.broadcasted_iota(jnp.int32, (LANES, SB_BLOCK), 0) < SB_HEAD_DIM
    s_idx = lax.broadcasted_iota(jnp.int32, (SB_BLOCK, 2 * SB_BLOCK), 0)
    t_idx = lax.broadcasted_iota(jnp.int32, (SB_BLOCK, 2 * SB_BLOCK), 1) % SB_BLOCK
    strictly_causal = s_idx < t_idx

    @pl.when(g == 0)
    def _():
        acc_ref[...] = jnp.zeros_like(acc_ref)

    def block_diag_heads(x):
        zero = jnp.zeros_like(x)
        return jnp.concatenate([jnp.where(top_rows, x, zero), jnp.where(top_rows, zero, x)], axis=1)

    for c in range(SB_QBLOCKS):
        q_t = q_ref[c * SB_BLOCK:(c + 1) * SB_BLOCK, :].astype(F32).T
        for p in range(pairs):
            qbd_ref[c, p] = block_diag_heads(q_t[cols[p], :].astype(BF16))

    def scores(blocks):
        chains = [(c, j, ok, p) for c, j, ok in blocks for p in range(pairs)]
        z = [_dot(k_ref[pl.ds(pl.multiple_of(j * SB_BLOCK, SB_BLOCK), SB_BLOCK), cols[p]], qbd_ref[c, p])
             for c, j, ok, p in chains]
        return chains, z

    def log_weights(z, masked):
        log_beta, first_row, hl = [], [], []
        for zc in z:
            lb = jnp.minimum(zc, 0.0) - jnp.log2(1.0 + jnp.exp2(-jnp.abs(zc)))
            lm = lb - zc
            if masked:
                lm = jnp.where(strictly_causal, lm, 0.0)
            hi = lm.astype(BF16)
            lo = (lm - hi.astype(F32)).astype(BF16)
            log_beta.append(lb)
            first_row.append(lm[0:1, :])
            hl.append(jnp.concatenate([hi, lo], axis=0))
        suffix = [_dot(tri_ref[...], x) for x in hl]
        total = [sf[0:1, :] + fr for sf, fr in zip(suffix, first_row)]
        return log_beta, suffix, total

    def weights(chains, log_beta, suffix, total, r, masked):
        a_t = []
        for i, (c, j, ok, p) in enumerate(chains):
            r_in = r[c, p] if ok is True else jnp.where(ok, r[c, p], -jnp.inf)
            a = jnp.exp2(log_beta[i] + suffix[i] + r_in)
            if masked:
                a = jnp.where(strictly_causal, a, 0.0)
            a = a.astype(BF16)
            a_t.append(jnp.concatenate([a[:, :SB_BLOCK], a[:, SB_BLOCK:]], axis=0))
            r[c, p] = r[c, p] + (total[i] if ok is True else jnp.where(ok, total[i], 0.0))
        return a_t

    def values(j, p, a_ts):
        v_bd = block_diag_heads(vt_ref[j, cols[p], :])
        pv = _dot(v_bd, a_ts[0] if len(a_ts) == 1 else jnp.concatenate(a_ts, axis=1))
        return [pv[:, i * SB_BLOCK:(i + 1) * SB_BLOCK] for i in range(len(a_ts))]

    def sweep(waves, fresh=False, fillers=()):
        r = {}
        for blocks, _ in waves:
            for c, _, _ in blocks:
                for p in range(pairs):
                    if (c, p) not in r:
                        r[c, p] = jnp.zeros((1, 2 * SB_BLOCK), F32) if fresh else r_ref[slot, c, p]
        out = {}

        def add(c, p, x):
            out[c, p] = x if (c, p) not in out else out[c, p] + x

        staged, waiting = [], {}
        for w in range(len(waves) + 3):
            if w < len(waves):
                blocks, masked = waves[w]
                chains, z = scores(blocks)
                staged.append([chains, z, masked])
            if 1 <= w <= len(waves):
                item = staged[w - 1]
                item[1:2] = [log_weights(item[1], item[2])]
            if 2 <= w:
                done = w - 2
                if done < len(waves):
                    chains, (log_beta, suffix, total), masked = staged[done]
                    a_t = weights(chains, log_beta, suffix, total, r, masked)
                else:
                    chains, a_t = [], []
                for (c, j, ok, p), a in zip(chains, a_t):
                    partner = waiting.pop((c - 1, done - 1, p), None)
                    if partner is not None:
                        first, second = values(j, p, [partner[1], a])
                        add(c - 1, p, first)
                        add(c, p, second)
                    elif c + 1 < SB_QBLOCKS and done + 1 < len(waves):
                        waiting[c, done, p] = (j, a)
                    else:
                        add(c, p, values(j, p, [a])[0])
                for (c, d, p) in [key for key in waiting if key[1] < done]:
                    j, a = waiting.pop((c, d, p))
                    add(c, p, values(j, p, [a])[0])
            if w < len(waves) + 2:
                for filler in fillers[w:w + 1]:
                    filler()
        bound = None
        for (c, p), x in out.items():
            acc_ref[slot, c, p] = x if fresh else acc_ref[slot, c, p] + x
            r_ref[slot, c, p] = r[c, p]
            bound = r[c, p] if bound is None else jnp.maximum(bound, r[c, p])
        bound = jnp.max(bound)
        for filler in fillers[len(waves) + 2:]:
            filler()
        return bound

    every = range(SB_QBLOCKS)

    def left(m):
        return ([(c, jnp.maximum(base + c - m, 0), base + c - m >= 0) for c in every], False)

    def finished_attention(acc_slot):
        rows = []
        for c in every:
            row = []
            for p in range(pairs):
                acc = acc_ref[acc_slot, c, p]
                sq = acc * acc
                ms0 = jnp.sum(sq[:SB_HEAD_DIM], axis=0, keepdims=True)
                ms1 = jnp.sum(sq[SB_HEAD_DIM:], axis=0, keepdims=True)
                ms = jnp.where(top_rows, ms0, ms1) * (1.0 / SB_HEAD_DIM)
                normed = (acc * lax.rsqrt(ms + EPS)).T
                row.append((normed * hg_ref[:, cols[p]]).astype(BF16))
            rows.append(jnp.concatenate(row, axis=1))
        return jnp.concatenate(rows, axis=0)

    tail = _tail_stages(x_ref, a_ref, finished_attention(1 - slot), kt_ref, vm_ref, wout_ref, gc_ref,
                        wcq_ref, wco_ref, gf_ref, w1_ref, w2_ref, gfin_ref, o_ref, ff_chunk)
    bound = sweep([([(c, base + c, True) for c in every], True), left(1), left(2)], fresh=True,
                  fillers=tail)

    def cond(carry):
        m, bound = carry
        return jnp.logical_and(m < base + SB_QBLOCKS, bound >= LOG2_ZERO)

    def body(carry):
        m, _ = carry
        return m + 1, sweep([left(m)])

    lax.while_loop(cond, body, (jnp.int32(3), bound))


def _sb_tail(q, k, vt, tri, hg, x, a, kt, vm, w_out, gc, w_cq, w_co, gf, w1, w2, gfin, *, ff_chunk=1024):
    b, s, d = x.shape
    width = q.shape[-1]
    pairs = width // LANES
    m = vm.shape[1]
    tm = SB_QBLOCKS * SB_BLOCK
    assert s % tm == 0
    nt = s // tm
    n = b * nt
    att = lambda g: jnp.minimum(g, n - 1)
    fin = lambda g: jnp.maximum(g - 1, 0)
    tile = lambda w, t: pl.BlockSpec((None, tm, w), lambda g: (t(g) // nt, t(g) % nt, 0))
    per_seq = lambda shape, t: pl.BlockSpec((None,) + shape, lambda g: (t(g) // nt,) + (0,) * len(shape))
    return pl.pallas_call(
        functools.partial(_sb_tail_kernel, tiles_per_seq=nt, ff_chunk=ff_chunk),
        grid=(n + 1,),
        in_specs=[tile(width, att), per_seq((s, width), att), per_seq(vt.shape[1:], att),
                  _resident(tri.shape), _resident(hg.shape),
                  tile(d, fin), tile(GM_WIDTH, fin), per_seq((d, m), fin), per_seq((m, d), fin),
                  _resident(w_out.shape), _resident(gc.shape), _resident(w_cq.shape),
                  _resident(w_co.shape), _resident(gf.shape), _resident(w1.shape),
                  _resident(w2.shape), _resident(gfin.shape)],
        out_specs=tile(d, fin),
        out_shape=jax.ShapeDtypeStruct((b, s, d), x.dtype),
        scratch_shapes=[pltpu.VMEM((SB_QBLOCKS, pairs, LANES, 2 * SB_BLOCK), BF16),
                        pltpu.VMEM((2, SB_QBLOCKS, pairs, 1, 2 * SB_BLOCK), F32),
                        pltpu.VMEM((2, SB_QBLOCKS, pairs, LANES, SB_BLOCK), F32)],
        compiler_params=pltpu.CompilerParams(
            dimension_semantics=("arbitrary",), vmem_limit_bytes=VMEM_LIMIT),
        name="sb_tail",
    )(q, k, vt, tri, hg, x, a, kt, vm, w_out, gc, w_cq, w_co, gf, w1, w2, gfin)


def _suffix_sum_matrix():
    s = lax.broadcasted_iota(jnp.int32, (SB_BLOCK, SB_BLOCK), 0)
    j = lax.broadcasted_iota(jnp.int32, (SB_BLOCK, SB_BLOCK), 1)
    upper = (j > s).astype(BF16)
    return jnp.concatenate([upper, upper], axis=1)


def kernel(x, mem, norm_mix_g, w_in, gm_v_norm_g, w_spatial, b_spatial, head_norm_g, w_out,
           norm_cross_g, norm_mem_g, w_cq, w_ckv, w_co, norm_ffn_g, w_ff1, w_ff2, norm_final_g):
    assert w_in.shape[0] == 1, "single trunk layer only"
    tril = jnp.tril(jnp.ones((CHUNK, CHUNK), dtype=bool))
    row = lambda g: g.reshape(1, -1).astype(F32)
    wsp = jnp.where(tril[None], w_spatial[0], 0.0).astype(BF16)
    bsp = jnp.broadcast_to(b_spatial[0][:, :, None], (GM_GROUPS, CHUNK, GM_DIM)).astype(F32)
    hg = row(head_norm_g[0])
    a, q, k, vt = _mix_in(x, row(norm_mix_g[0]), w_in[0].astype(BF16), row(gm_v_norm_g[0]),
                          wsp, bsp, hg[:, :GM_WIDTH])
    kt, vm = _mem_kv(mem, row(norm_mem_g[0]), w_ckv[0].astype(BF16))
    return _sb_tail(q, k, vt, _suffix_sum_matrix(), hg[:, GM_WIDTH:], x, a, kt, vm,
                    w_out[0].astype(BF16), row(norm_cross_g[0]), w_cq[0].astype(BF16),
                    w_co[0].astype(BF16), row(norm_ffn_g[0]), w_ff1[0].astype(BF16),
                    w_ff2[0].astype(BF16), row(norm_final_g))
```

```python
import functools
import math

import jax
import jax.numpy as jnp
from jax import lax
from jax.experimental import pallas as pl
from jax.experimental.pallas import tpu as pltpu

EPS = 1e-6
CHUNK = 128
GM_GROUPS = 4
GM_DIM = 128
GM_WIDTH = GM_GROUPS * GM_DIM
SB_HEADS = 8
SB_HEAD_DIM = 64
SB_WIDTH = SB_HEADS * SB_HEAD_DIM
SB_BLOCK = 128
SB_QBLOCKS = 2
SB_UNITS = 2
X_HEADS = 4
MIX_PARTS = 2

LANES = 128
VMEM_LIMIT = 56 * 1024 * 1024

LOG2_ZERO = -151.0
LOG2_E = 1.4426950408889634

F32 = jnp.float32
BF16 = jnp.bfloat16


def _dot(a, b):
    return jnp.dot(a, b, preferred_element_type=F32)


def _inv_rms(x):
    return lax.rsqrt(jnp.mean(x * x, axis=-1, keepdims=True) + EPS)


def _rms(x, g):
    return x * _inv_rms(x) * g


def _resident(shape):
    zeros = (0,) * len(shape)
    return pl.BlockSpec(shape, lambda *_: zeros, pipeline_mode=pl.Buffered(1))


def _mix_in_kernel(x_ref, g_ref, win_ref, gvg_ref, wsp_ref, bsp_ref, hg_ref,
                   a_ref, q_ref, k_ref, vt_ref, *, tm):
    base = 2 * GM_WIDTH
    scale = LOG2_E / math.sqrt(SB_HEAD_DIM)
    part = tm // MIX_PARTS
    for r0 in range(0, tm, part):
        x = x_ref[r0:r0 + part, :]
        xg = (x * g_ref[...]).astype(BF16)
        inv = _inv_rms(x)
        u = jax.nn.gelu(_dot(xg, win_ref[:, 0:GM_WIDTH]) * inv)
        gv = jax.nn.gelu(_dot(xg, win_ref[:, GM_WIDTH:2 * GM_WIDTH]) * inv)
        q_ref[r0:r0 + part, :] = (_dot(xg, win_ref[:, base:base + SB_WIDTH]) * (inv * scale)).astype(BF16)
        k_ref[r0:r0 + part, :] = (_dot(xg, win_ref[:, base + SB_WIDTH:base + 2 * SB_WIDTH]) * inv).astype(BF16)
        v = _dot(xg, win_ref[:, base + 2 * SB_WIDTH:base + 3 * SB_WIDTH]) * inv
        for g in range(GM_GROUPS):
            cols = slice(g * GM_DIM, (g + 1) * GM_DIM)
            gvn = _rms(gv[:, cols], gvg_ref[:, cols]).astype(BF16)
            for c in range(part // CHUNK):
                rows = slice(c * CHUNK, (c + 1) * CHUNK)
                mixed = _dot(wsp_ref[g], gvn[rows]) + bsp_ref[g]
                gated = u[rows, cols] * mixed
                a_ref[r0 + c * CHUNK:r0 + (c + 1) * CHUNK, cols] = _rms(gated, hg_ref[:, cols]).astype(BF16)
        for c in range(part // SB_BLOCK):
            vt_ref[r0 // SB_BLOCK + c] = v[c * SB_BLOCK:(c + 1) * SB_BLOCK, :].T.astype(BF16)


def _mix_in(x, g, w_in, gvg, wsp, bsp, hg, *, tm=1024):
    b, s, d = x.shape
    assert s % tm == 0
    tok = lambda width: pl.BlockSpec((None, tm, width), lambda i, j: (i, j, 0))
    out = jax.ShapeDtypeStruct((b, s, GM_WIDTH), BF16)
    vt_out = jax.ShapeDtypeStruct((b, s // SB_BLOCK, SB_WIDTH, SB_BLOCK), BF16)
    vt_spec = pl.BlockSpec((None, tm // SB_BLOCK, SB_WIDTH, SB_BLOCK), lambda i, j: (i, j, 0, 0))
    return pl.pallas_call(
        functools.partial(_mix_in_kernel, tm=tm),
        grid=(b, s // tm),
        in_specs=[tok(d), _resident(g.shape), _resident(w_in.shape), _resident(gvg.shape),
                  _resident(wsp.shape), _resident(bsp.shape), _resident(hg.shape)],
        out_specs=[tok(GM_WIDTH)] * 3 + [vt_spec],
        out_shape=[out] * 3 + [vt_out],
        compiler_params=pltpu.CompilerParams(
            dimension_semantics=("parallel", "parallel"), vmem_limit_bytes=VMEM_LIMIT),
        name="mix_in",
    )(x, g, w_in, gvg, wsp, bsp, hg)


def _mem_kv_kernel(mem_ref, g_ref, w_ref, kt_ref, v_ref):
    d = mem_ref.shape[-1]
    mn = _rms(mem_ref[...], g_ref[...]).astype(BF16)
    kt_ref[...] = _dot(mn, w_ref[:, 0:d]).T.astype(BF16)
    v_ref[...] = _dot(mn, w_ref[:, d:2 * d]).astype(BF16)


def _mem_kv(mem, g, w_ckv):
    b, m, d = mem.shape
    return pl.pallas_call(
        _mem_kv_kernel,
        grid=(b,),
        in_specs=[pl.BlockSpec((None, m, d), lambda i: (i, 0, 0)),
                  _resident(g.shape), _resident(w_ckv.shape)],
        out_specs=[pl.BlockSpec((None, d, m), lambda i: (i, 0, 0)),
                   pl.BlockSpec((None, m, d), lambda i: (i, 0, 0))],
        out_shape=[jax.ShapeDtypeStruct((b, d, m), BF16), jax.ShapeDtypeStruct((b, m, d), BF16)],
        compiler_params=pltpu.CompilerParams(
            dimension_semantics=("parallel",), vmem_limit_bytes=VMEM_LIMIT),
        name="mem_kv",
    )(mem, g, w_ckv)


def _tail_stages(x_ref, a_ref, sb, kt_ref, v_ref, wout_ref, gc_ref, wcq_ref, wco_ref, gf_ref, w1_ref,
                 w2_ref, gfin_ref, o_ref, ff_chunk):
    d = x_ref.shape[-1]
    hd = d // X_HEADS
    d_ff = w1_ref.shape[1]
    st = {}

    def attention():
        h = x_ref[...] + _dot(jnp.concatenate([a_ref[...], sb], axis=1), wout_ref[...])
        qx = _dot((h * gc_ref[...]).astype(BF16), wcq_ref[...])
        qx = (qx * (_inv_rms(h) * (1.0 / math.sqrt(hd)))).astype(BF16)
        heads = []
        for i in range(X_HEADS):
            cols = slice(i * hd, (i + 1) * hd)
            sc = _dot(qx[:, cols], kt_ref[cols, :])
            e = jnp.exp(sc - jnp.max(sc, axis=-1, keepdims=True))
            o = _dot(e.astype(BF16), v_ref[:, cols]) / jnp.sum(e, axis=-1, keepdims=True)
            heads.append(o.astype(BF16))
        st["h"] = h + _dot(jnp.concatenate(heads, axis=1), wco_ref[...])
        st["hg"] = (st["h"] * gf_ref[...]).astype(BF16)
        st["inv2"] = jnp.square(_inv_rms(st["h"]))
        st["ff"] = None

    def mlp_chunk(c):
        cols = slice(c * ff_chunk, (c + 1) * ff_chunk)
        r = jnp.maximum(_dot(st["hg"], w1_ref[:, cols]), 0.0)
        part = _dot((r * r).astype(BF16), w2_ref[cols, :])
        st["ff"] = part if st["ff"] is None else st["ff"] + part

    def finish():
        o_ref[...] = _rms(st["h"] + st["ff"] * st["inv2"], gfin_ref[...])

    return [attention] + [functools.partial(mlp_chunk, c) for c in range(d_ff // ff_chunk)] + [finish]


def _sb_tail_kernel(q_ref, k_ref, vt_ref, tri_ref, hg_ref,
                    x_ref, a_ref, kt_ref, vm_ref, wout_ref, gc_ref, wcq_ref, wco_ref, gf_ref,
                    w1_ref, w2_ref, gfin_ref, o_ref, qbd_ref, r_ref, acc_ref,
                    *, tiles_per_seq, ff_chunk):
    g = pl.program_id(0)
    last = pl.num_programs(0) - 2
    first_tile = jnp.minimum(g, last) * SB_UNITS
    slot = g % 2
    pairs = q_ref.shape[-1] // LANES
    tile_rows = SB_QBLOCKS * SB_BLOCK
    cols = [slice(p * LANES, (p + 1) * LANES) for p in range(pairs)]
    top_rows = lax.broadcasted_iota(jnp.int32, (LANES, SB_BLOCK), 0) < SB_HEAD_DIM
    s_idx = lax.broadcasted_iota(jnp.int32, (SB_BLOCK, 2 * SB_BLOCK), 0)
    t_idx = lax.broadcasted_iota(jnp.int32, (SB_BLOCK, 2 * SB_BLOCK), 1) % SB_BLOCK
    strictly_causal = s_idx < t_idx
    every = range(SB_QBLOCKS)

    @pl.when(g == 0)
    def _():
        acc_ref[...] = jnp.zeros_like(acc_ref)

    def block_diag_heads(x):
        zero = jnp.zeros_like(x)
        return jnp.concatenate([jnp.where(top_rows, x, zero), jnp.where(top_rows, zero, x)], axis=1)

    def log_weights(z, masked):
        log_beta, first_row, hl = [], [], []
        for zc in z:
            lb = jnp.minimum(zc, 0.0) - jnp.log2(1.0 + jnp.exp2(-jnp.abs(zc)))
            lm = lb - zc
            if masked:
                lm = jnp.where(strictly_causal, lm, 0.0)
            hi = lm.astype(BF16)
            lo = (lm - hi.astype(F32)).astype(BF16)
            log_beta.append(lb)
            first_row.append(lm[0:1, :])
            hl.append(jnp.concatenate([hi, lo], axis=0))
        suffix = [_dot(tri_ref[...], x) for x in hl]
        total = [sf[0:1, :] + fr for sf, fr in zip(suffix, first_row)]
        return log_beta, suffix, total

    def weights(chains, log_beta, suffix, total, r, masked):
        a_t = []
        for i, (c, j, ok, p) in enumerate(chains):
            r_in = r[c, p] if ok is True else jnp.where(ok, r[c, p], -jnp.inf)
            a = jnp.exp2(log_beta[i] + suffix[i] + r_in)
            if masked:
                a = jnp.where(strictly_causal, a, 0.0)
            a = a.astype(BF16)
            a_t.append(jnp.concatenate([a[:, :SB_BLOCK], a[:, SB_BLOCK:]], axis=0))
            r[c, p] = r[c, p] + (total[i] if ok is True else jnp.where(ok, total[i], 0.0))
        return a_t

    def values(j, p, a_ts):
        v_bd = block_diag_heads(vt_ref[j, cols[p], :])
        pv = _dot(v_bd, a_ts[0] if len(a_ts) == 1 else jnp.concatenate(a_ts, axis=1))
        return [pv[:, i * SB_BLOCK:(i + 1) * SB_BLOCK] for i in range(len(a_ts))]

    def attend_tile(u):
        base = ((first_tile + u) % tiles_per_seq) * SB_QBLOCKS
        row0 = u * tile_rows

        for c in every:
            q_t = q_ref[row0 + c * SB_BLOCK:row0 + (c + 1) * SB_BLOCK, :].astype(F32).T
            for p in range(pairs):
                qbd_ref[u, c, p] = block_diag_heads(q_t[cols[p], :].astype(BF16))

        def scores(blocks):
            chains = [(c, j, ok, p) for c, j, ok in blocks for p in range(pairs)]
            z = [_dot(k_ref[pl.ds(pl.multiple_of(j * SB_BLOCK, SB_BLOCK), SB_BLOCK), cols[p]],
                      qbd_ref[u, c, p]) for c, j, ok, p in chains]
            return chains, z

        def sweep(waves, fresh=False, fillers=()):
            r = {}
            for blocks, _ in waves:
                for c, _, _ in blocks:
                    for p in range(pairs):
                        if (c, p) not in r:
                            r[c, p] = (jnp.zeros((1, 2 * SB_BLOCK), F32) if fresh
                                       else r_ref[slot, u, c, p])
            out = {}

            def add(c, p, x):
                out[c, p] = x if (c, p) not in out else out[c, p] + x

            staged, waiting = [], {}
            for w in range(len(waves) + 3):
                if w < len(waves):
                    blocks, masked = waves[w]
                    chains, z = scores(blocks)
                    staged.append([chains, z, masked])
                if 1 <= w <= len(waves):
                    item = staged[w - 1]
                    item[1:2] = [log_weights(item[1], item[2])]
                if 2 <= w:
                    done = w - 2
                    if done < len(waves):
                        chains, (log_beta, suffix, total), masked = staged[done]
                        a_t = weights(chains, log_beta, suffix, total, r, masked)
                    else:
                        chains, a_t = [], []
                    for (c, j, ok, p), a in zip(chains, a_t):
                        partner = waiting.pop((c - 1, done - 1, p), None)
                        if partner is not None:
                            first, second = values(j, p, [partner[1], a])
                            add(c - 1, p, first)
                            add(c, p, second)
                        elif c + 1 < SB_QBLOCKS and done + 1 < len(waves):
                            waiting[c, done, p] = (j, a)
                        else:
                            add(c, p, values(j, p, [a])[0])
                    for (c, d, p) in [key for key in waiting if key[1] < done]:
                        j, a = waiting.pop((c, d, p))
                        add(c, p, values(j, p, [a])[0])
                if w < len(waves) + 2:
                    for filler in fillers[w:w + 1]:
                        filler()
            bound = None
            for (c, p), x in out.items():
                acc_ref[slot, u, c, p] = x if fresh else acc_ref[slot, u, c, p] + x
                r_ref[slot, u, c, p] = r[c, p]
                bound = r[c, p] if bound is None else jnp.maximum(bound, r[c, p])
            bound = jnp.max(bound)
            for filler in fillers[len(waves) + 2:]:
                filler()
            return bound

        def left(m):
            return ([(c, jnp.maximum(base + c - m, 0), base + c - m >= 0) for c in every], False)

        diagonal = ([(c, base + c, True) for c in every], True)
        return sweep, diagonal, left, base

    def finished_attention(u):
        rows = []
        for c in every:
            row = []
            for p in range(pairs):
                acc = acc_ref[1 - slot, u, c, p]
                sq = acc * acc
                ms0 = jnp.sum(sq[:SB_HEAD_DIM], axis=0, keepdims=True)
                ms1 = jnp.sum(sq[SB_HEAD_DIM:], axis=0, keepdims=True)
                ms = jnp.where(top_rows, ms0, ms1) * (1.0 / SB_HEAD_DIM)
                normed = (acc * lax.rsqrt(ms + EPS)).T
                row.append((normed * hg_ref[:, cols[p]]).astype(BF16))
            rows.append(jnp.concatenate(row, axis=1))
        return jnp.concatenate(rows, axis=0)

    pending = []
    for u in range(SB_UNITS):
        rows = slice(u * tile_rows, (u + 1) * tile_rows)
        tail = _tail_stages(x_ref.at[rows, :], a_ref.at[rows, :], finished_attention(u), kt_ref, vm_ref,
                            wout_ref, gc_ref, wcq_ref, wco_ref, gf_ref, w1_ref, w2_ref, gfin_ref,
                            o_ref.at[rows, :], ff_chunk)
        sweep, diagonal, left, base = attend_tile(u)
        pending.append((sweep, left, base, sweep([diagonal, left(1), left(2)], fresh=True, fillers=tail)))

    for sweep, left, base, bound in pending:
        def cond(carry, base=base):
            m, bound = carry
            return jnp.logical_and(m < base + SB_QBLOCKS, bound >= LOG2_ZERO)

        def body(carry, sweep=sweep, left=left):
            m, _ = carry
            return m + 1, sweep([left(m)])

        lax.while_loop(cond, body, (jnp.int32(3), bound))


def _sb_tail(q, k, vt, tri, hg, x, a, kt, vm, w_out, gc, w_cq, w_co, gf, w1, w2, gfin, *, ff_chunk=1024):
    b, s, d = x.shape
    width = q.shape[-1]
    pairs = width // LANES
    m = vm.shape[1]
    tile_rows = SB_QBLOCKS * SB_BLOCK
    tm = SB_UNITS * tile_rows
    assert s % tm == 0
    nt = s // tm
    n = b * nt
    att = lambda g: jnp.minimum(g, n - 1)
    fin = lambda g: jnp.maximum(g - 1, 0)
    tile = lambda w, t: pl.BlockSpec((None, tm, w), lambda g: (t(g) // nt, t(g) % nt, 0))
    per_seq = lambda shape, t: pl.BlockSpec((None,) + shape, lambda g: (t(g) // nt,) + (0,) * len(shape))
    return pl.pallas_call(
        functools.partial(_sb_tail_kernel, tiles_per_seq=s // tile_rows, ff_chunk=ff_chunk),
        grid=(n + 1,),
        in_specs=[tile(width, att), per_seq((s, width), att), per_seq(vt.shape[1:], att),
                  _resident(tri.shape), _resident(hg.shape),
                  tile(d, fin), tile(GM_WIDTH, fin), per_seq((d, m), fin), per_seq((m, d), fin),
                  _resident(w_out.shape), _resident(gc.shape), _resident(w_cq.shape),
                  _resident(w_co.shape), _resident(gf.shape), _resident(w1.shape),
                  _resident(w2.shape), _resident(gfin.shape)],
        out_specs=tile(d, fin),
        out_shape=jax.ShapeDtypeStruct((b, s, d), x.dtype),
        scratch_shapes=[pltpu.VMEM((SB_UNITS, SB_QBLOCKS, pairs, LANES, 2 * SB_BLOCK), BF16),
                        pltpu.VMEM((2, SB_UNITS, SB_QBLOCKS, pairs, 1, 2 * SB_BLOCK), F32),
                        pltpu.VMEM((2, SB_UNITS, SB_QBLOCKS, pairs, LANES, SB_BLOCK), F32)],
        compiler_params=pltpu.CompilerParams(
            dimension_semantics=("arbitrary",), vmem_limit_bytes=VMEM_LIMIT),
        name="sb_tail",
    )(q, k, vt, tri, hg, x, a, kt, vm, w_out, gc, w_cq, w_co, gf, w1, w2, gfin)


def _suffix_sum_matrix():
    s = lax.broadcasted_iota(jnp.int32, (SB_BLOCK, SB_BLOCK), 0)
    j = lax.broadcasted_iota(jnp.int32, (SB_BLOCK, SB_BLOCK), 1)
    upper = (j > s).astype(BF16)
    return jnp.concatenate([upper, upper], axis=1)


def kernel(x, mem, norm_mix_g, w_in, gm_v_norm_g, w_spatial, b_spatial, head_norm_g, w_out,
           norm_cross_g, norm_mem_g, w_cq, w_ckv, w_co, norm_ffn_g, w_ff1, w_ff2, norm_final_g):
    assert w_in.shape[0] == 1, "single trunk layer only"
    tril = jnp.tril(jnp.ones((CHUNK, CHUNK), dtype=bool))
    row = lambda g: g.reshape(1, -1).astype(F32)
    wsp = jnp.where(tril[None], w_spatial[0], 0.0).astype(BF16)
    bsp = jnp.broadcast_to(b_spatial[0][:, :, None], (GM_GROUPS, CHUNK, GM_DIM)).astype(F32)
    hg = row(head_norm_g[0])
    a, q, k, vt = _mix_in(x, row(norm_mix_g[0]), w_in[0].astype(BF16), row(gm_v_norm_g[0]),
                          wsp, bsp, hg[:, :GM_WIDTH])
    kt, vm = _mem_kv(mem, row(norm_mem_g[0]), w_ckv[0].astype(BF16))
    return _sb_tail(q, k, vt, _suffix_sum_matrix(), hg[:, GM_WIDTH:], x, a, kt, vm,
                    w_out[0].astype(BF16), row(norm_cross_g[0]), w_cq[0].astype(BF16),
                    w_co[0].astype(BF16), row(norm_ffn_g[0]), w_ff1[0].astype(BF16),
                    w_ff2[0].astype(BF16), row(norm_final_g))
```

```python
import functools
import math

import jax
import jax.numpy as jnp
from jax import lax
from jax.experimental import pallas as pl
from jax.experimental.pallas import tpu as pltpu

EPS = 1e-6
CHUNK = 128
GM_GROUPS = 4
GM_DIM = 128
GM_WIDTH = GM_GROUPS * GM_DIM
SB_HEADS = 8
SB_HEAD_DIM = 64
SB_WIDTH = SB_HEADS * SB_HEAD_DIM
SB_BLOCK = 128
SB_QBLOCKS = 2
SB_UNITS = 2
X_HEADS = 4
MIX_PARTS = 2

LANES = 128
VMEM_LIMIT = 56 * 1024 * 1024

LOG2_ZERO = -151.0
LOG2_E = 1.4426950408889634

F32 = jnp.float32
BF16 = jnp.bfloat16


def _dot(a, b):
    return jnp.dot(a, b, preferred_element_type=F32)


def _inv_rms(x):
    return lax.rsqrt(jnp.mean(x * x, axis=-1, keepdims=True) + EPS)


def _rms(x, g):
    return x * _inv_rms(x) * g


def _resident(shape):
    zeros = (0,) * len(shape)
    return pl.BlockSpec(shape, lambda *_: zeros, pipeline_mode=pl.Buffered(1))


def _mix_in_kernel(x_ref, g_ref, win_ref, gvg_ref, wsp_ref, bsp_ref, hg_ref,
                   a_ref, q_ref, k_ref, vt_ref, *, tm):
    base = 2 * GM_WIDTH
    scale = LOG2_E / math.sqrt(SB_HEAD_DIM)
    part = tm // MIX_PARTS
    for r0 in range(0, tm, part):
        x = x_ref[r0:r0 + part, :]
        xg = (x * g_ref[...]).astype(BF16)
        inv = _inv_rms(x)
        u = jax.nn.gelu(_dot(xg, win_ref[:, 0:GM_WIDTH]) * inv)
        gv = jax.nn.gelu(_dot(xg, win_ref[:, GM_WIDTH:2 * GM_WIDTH]) * inv)
        q_ref[r0:r0 + part, :] = (_dot(xg, win_ref[:, base:base + SB_WIDTH]) * (inv * scale)).astype(BF16)
        k_ref[r0:r0 + part, :] = (_dot(xg, win_ref[:, base + SB_WIDTH:base + 2 * SB_WIDTH]) * inv).astype(BF16)
        v = _dot(xg, win_ref[:, base + 2 * SB_WIDTH:base + 3 * SB_WIDTH]) * inv
        for g in range(GM_GROUPS):
            cols = slice(g * GM_DIM, (g + 1) * GM_DIM)
            gvn = _rms(gv[:, cols], gvg_ref[:, cols]).astype(BF16)
            for c in range(part // CHUNK):
                rows = slice(c * CHUNK, (c + 1) * CHUNK)
                mixed = _dot(wsp_ref[g], gvn[rows]) + bsp_ref[g]
                gated = u[rows, cols] * mixed
                a_ref[r0 + c * CHUNK:r0 + (c + 1) * CHUNK, cols] = _rms(gated, hg_ref[:, cols]).astype(BF16)
        for c in range(part // SB_BLOCK):
            vt_ref[r0 // SB_BLOCK + c] = v[c * SB_BLOCK:(c + 1) * SB_BLOCK, :].T.astype(BF16)


def _mix_in(x, g, w_in, gvg, wsp, bsp, hg, *, tm=1024):
    b, s, d = x.shape
    assert s % tm == 0
    tok = lambda width: pl.BlockSpec((None, tm, width), lambda i, j: (i, j, 0))
    out = jax.ShapeDtypeStruct((b, s, GM_WIDTH), BF16)
    vt_out = jax.ShapeDtypeStruct((b, s // SB_BLOCK, SB_WIDTH, SB_BLOCK), BF16)
    vt_spec = pl.BlockSpec((None, tm // SB_BLOCK, SB_WIDTH, SB_BLOCK), lambda i, j: (i, j, 0, 0))
    return pl.pallas_call(
        functools.partial(_mix_in_kernel, tm=tm),
        grid=(b, s // tm),
        in_specs=[tok(d), _resident(g.shape), _resident(w_in.shape), _resident(gvg.shape),
                  _resident(wsp.shape), _resident(bsp.shape), _resident(hg.shape)],
        out_specs=[tok(GM_WIDTH)] * 3 + [vt_spec],
        out_shape=[out] * 3 + [vt_out],
        compiler_params=pltpu.CompilerParams(
            dimension_semantics=("parallel", "parallel"), vmem_limit_bytes=VMEM_LIMIT),
        name="mix_in",
    )(x, g, w_in, gvg, wsp, bsp, hg)


def _mem_kv_kernel(mem_ref, g_ref, w_ref, kt_ref, v_ref):
    nb, m, d = mem_ref.shape
    mem = mem_ref[...].reshape(nb * m, d)
    mn = _rms(mem, g_ref[...]).astype(BF16)
    k = _dot(mn, w_ref[:, 0:d])
    for i in range(nb):
        kt_ref[i] = k[i * m:(i + 1) * m, :].T.astype(BF16)
    v_ref[...] = _dot(mn, w_ref[:, d:2 * d]).astype(BF16).reshape(nb, m, d)


def _mem_kv(mem, g, w_ckv, *, nb=4):
    b, m, d = mem.shape
    assert b % nb == 0
    return pl.pallas_call(
        _mem_kv_kernel,
        grid=(b // nb,),
        in_specs=[pl.BlockSpec((nb, m, d), lambda i: (i, 0, 0)),
                  _resident(g.shape), _resident(w_ckv.shape)],
        out_specs=[pl.BlockSpec((nb, d, m), lambda i: (i, 0, 0)),
                   pl.BlockSpec((nb, m, d), lambda i: (i, 0, 0))],
        out_shape=[jax.ShapeDtypeStruct((b, d, m), BF16), jax.ShapeDtypeStruct((b, m, d), BF16)],
        compiler_params=pltpu.CompilerParams(
            dimension_semantics=("parallel",), vmem_limit_bytes=VMEM_LIMIT),
        name="mem_kv",
    )(mem, g, w_ckv)


def _tail_stages(x_ref, a_ref, sb, kt_ref, v_ref, wout_ref, gc_ref, wcq_ref, wco_ref, gf_ref, w1_ref,
                 w2_ref, gfin_ref, o_ref, ff_chunk):
    d = x_ref.shape[-1]
    hd = d // X_HEADS
    d_ff = w1_ref.shape[1]
    st = {}

    def attention():
        h = x_ref[...] + _dot(jnp.concatenate([a_ref[...], sb], axis=1), wout_ref[...])
        qx = _dot((h * gc_ref[...]).astype(BF16), wcq_ref[...])
        qx = (qx * (_inv_rms(h) * (1.0 / math.sqrt(hd)))).astype(BF16)
        heads = []
        for i in range(X_HEADS):
            cols = slice(i * hd, (i + 1) * hd)
            sc = _dot(qx[:, cols], kt_ref[cols, :])
            e = jnp.exp(sc - jnp.max(sc, axis=-1, keepdims=True))
            o = _dot(e.astype(BF16), v_ref[:, cols]) / jnp.sum(e, axis=-1, keepdims=True)
            heads.append(o.astype(BF16))
        st["h"] = h + _dot(jnp.concatenate(heads, axis=1), wco_ref[...])
        st["hg"] = (st["h"] * gf_ref[...]).astype(BF16)
        st["inv2"] = jnp.square(_inv_rms(st["h"]))
        st["ff"] = None

    def mlp_chunk(c):
        cols = slice(c * ff_chunk, (c + 1) * ff_chunk)
        r = jnp.maximum(_dot(st["hg"], w1_ref[:, cols]), 0.0)
        part = _dot((r * r).astype(BF16), w2_ref[cols, :])
        st["ff"] = part if st["ff"] is None else st["ff"] + part

    def finish():
        o_ref[...] = _rms(st["h"] + st["ff"] * st["inv2"], gfin_ref[...])

    return [attention] + [functools.partial(mlp_chunk, c) for c in range(d_ff // ff_chunk)] + [finish]


def _sb_tail_kernel(q_ref, k_ref, vt_ref, tri_ref, hg_ref,
                    x_ref, a_ref, kt_ref, vm_ref, wout_ref, gc_ref, wcq_ref, wco_ref, gf_ref,
                    w1_ref, w2_ref, gfin_ref, o_ref, qbd_ref, r_ref, acc_ref,
                    *, tiles_per_seq, ff_chunk):
    g = pl.program_id(0)
    last = pl.num_programs(0) - 2
    first_tile = jnp.minimum(g, last) * SB_UNITS
    slot = g % 2
    pairs = q_ref.shape[-1] // LANES
    tile_rows = SB_QBLOCKS * SB_BLOCK
    cols = [slice(p * LANES, (p + 1) * LANES) for p in range(pairs)]
    top_rows = lax.broadcasted_iota(jnp.int32, (LANES, SB_BLOCK), 0) < SB_HEAD_DIM
    s_idx = lax.broadcasted_iota(jnp.int32, (SB_BLOCK, 2 * SB_BLOCK), 0)
    t_idx = lax.broadcasted_iota(jnp.int32, (SB_BLOCK, 2 * SB_BLOCK), 1) % SB_BLOCK
    strictly_causal = s_idx < t_idx
    every = range(SB_QBLOCKS)

    @pl.when(g == 0)
    def _():
        acc_ref[...] = jnp.zeros_like(acc_ref)

    def block_diag_heads(x):
        zero = jnp.zeros_like(x)
        return jnp.concatenate([jnp.where(top_rows, x, zero), jnp.where(top_rows, zero, x)], axis=1)

    def log_weights(z, masked):
        log_beta, first_row, hl = [], [], []
        for zc in z:
            lb = jnp.minimum(zc, 0.0) - jnp.log2(1.0 + jnp.exp2(-jnp.abs(zc)))
            lm = lb - zc
            if masked:
                lm = jnp.where(strictly_causal, lm, 0.0)
            hi = lm.astype(BF16)
            lo = (lm - hi.astype(F32)).astype(BF16)
            log_beta.append(lb)
            first_row.append(lm[0:1, :])
            hl.append(jnp.concatenate([hi, lo], axis=0))
        suffix = [_dot(tri_ref[...], x) for x in hl]
        total = [sf[0:1, :] + fr for sf, fr in zip(suffix, first_row)]
        return log_beta, suffix, total

    def weights(chains, log_beta, suffix, total, r, masked):
        a_t = []
        for i, (c, j, ok, p) in enumerate(chains):
            r_in = r[c, p] if ok is True else jnp.where(ok, r[c, p], -jnp.inf)
            a = jnp.exp2(log_beta[i] + suffix[i] + r_in)
            if masked:
                a = jnp.where(strictly_causal, a, 0.0)
            a = a.astype(BF16)
            a_t.append(jnp.concatenate([a[:, :SB_BLOCK], a[:, SB_BLOCK:]], axis=0))
            r[c, p] = r[c, p] + (total[i] if ok is True else jnp.where(ok, total[i], 0.0))
        return a_t

    def values(j, p, a_ts):
        v_bd = block_diag_heads(vt_ref[j, cols[p], :])
        pv = _dot(v_bd, a_ts[0] if len(a_ts) == 1 else jnp.concatenate(a_ts, axis=1))
        return [pv[:, i * SB_BLOCK:(i + 1) * SB_BLOCK] for i in range(len(a_ts))]

    def attend_tile(u):
        base = ((first_tile + u) % tiles_per_seq) * SB_QBLOCKS
        row0 = u * tile_rows

        for c in every:
            q_t = q_ref[row0 + c * SB_BLOCK:row0 + (c + 1) * SB_BLOCK, :].astype(F32).T
            for p in range(pairs):
                qbd_ref[u, c, p] = block_diag_heads(q_t[cols[p], :].astype(BF16))

        def scores(blocks):
            chains = [(c, j, ok, p) for c, j, ok in blocks for p in range(pairs)]
            z = [_dot(k_ref[pl.ds(pl.multiple_of(j * SB_BLOCK, SB_BLOCK), SB_BLOCK), cols[p]],
                      qbd_ref[u, c, p]) for c, j, ok, p in chains]
            return chains, z

        def sweep(waves, fresh=False, fillers=()):
            r = {}
            for blocks, _ in waves:
                for c, _, _ in blocks:
                    for p in range(pairs):
                        if (c, p) not in r:
                            r[c, p] = (jnp.zeros((1, 2 * SB_BLOCK), F32) if fresh
                                       else r_ref[slot, u, c, p])
            out = {}

            def add(c, p, x):
                out[c, p] = x if (c, p) not in out else out[c, p] + x

            staged, waiting = [], {}
            for w in range(len(waves) + 3):
                if w < len(waves):
                    blocks, masked = waves[w]
                    chains, z = scores(blocks)
                    staged.append([chains, z, masked])
                if 1 <= w <= len(waves):
                    item = staged[w - 1]
                    item[1:2] = [log_weights(item[1], item[2])]
                if 2 <= w:
                    done = w - 2
                    if done < len(waves):
                        chains, (log_beta, suffix, total), masked = staged[done]
                        a_t = weights(chains, log_beta, suffix, total, r, masked)
                    else:
                        chains, a_t = [], []
                    for (c, j, ok, p), a in zip(chains, a_t):
                        partner = waiting.pop((c - 1, done - 1, p), None)
                        if partner is not None:
                            first, second = values(j, p, [partner[1], a])
                            add(c - 1, p, first)
                            add(c, p, second)
                        elif c + 1 < SB_QBLOCKS and done + 1 < len(waves):
                            waiting[c, done, p] = (j, a)
                        else:
                            add(c, p, values(j, p, [a])[0])
                    for (c, d, p) in [key for key in waiting if key[1] < done]:
                        j, a = waiting.pop((c, d, p))
                        add(c, p, values(j, p, [a])[0])
                if w < len(waves) + 2:
                    for filler in fillers[w:w + 1]:
                        filler()
            bound = None
            for (c, p), x in out.items():
                acc_ref[slot, u, c, p] = x if fresh else acc_ref[slot, u, c, p] + x
                r_ref[slot, u, c, p] = r[c, p]
                bound = r[c, p] if bound is None else jnp.maximum(bound, r[c, p])
            bound = jnp.max(bound)
            for filler in fillers[len(waves) + 2:]:
                filler()
            return bound

        def left(m):
            return ([(c, jnp.maximum(base + c - m, 0), base + c - m >= 0) for c in every], False)

        diagonal = ([(c, base + c, True) for c in every], True)
        return sweep, diagonal, left, base

    def finished_attention(u):
        rows = []
        for c in every:
            row = []
            for p in range(pairs):
                acc = acc_ref[1 - slot, u, c, p]
                sq = acc * acc
                ms0 = jnp.sum(sq[:SB_HEAD_DIM], axis=0, keepdims=True)
                ms1 = jnp.sum(sq[SB_HEAD_DIM:], axis=0, keepdims=True)
                ms = jnp.where(top_rows, ms0, ms1) * (1.0 / SB_HEAD_DIM)
                normed = (acc * lax.rsqrt(ms + EPS)).T
                row.append((normed * hg_ref[:, cols[p]]).astype(BF16))
            rows.append(jnp.concatenate(row, axis=1))
        return jnp.concatenate(rows, axis=0)

    pending = []
    for u in range(SB_UNITS):
        rows = slice(u * tile_rows, (u + 1) * tile_rows)
        tail = _tail_stages(x_ref.at[rows, :], a_ref.at[rows, :], finished_attention(u), kt_ref, vm_ref,
                            wout_ref, gc_ref, wcq_ref, wco_ref, gf_ref, w1_ref, w2_ref, gfin_ref,
                            o_ref.at[rows, :], ff_chunk)
        sweep, diagonal, left, base = attend_tile(u)
        pending.append((sweep, left, base, sweep([diagonal, left(1), left(2)], fresh=True, fillers=tail)))

    for sweep, left, base, bound in pending:
        def cond(carry, base=base):
            m, bound = carry
            return jnp.logical_and(m < base + SB_QBLOCKS, bound >= LOG2_ZERO)

        def body(carry, sweep=sweep, left=left):
            m, _ = carry
            return m + 1, sweep([left(m)])

        lax.while_loop(cond, body, (jnp.int32(3), bound))


def _sb_tail(q, k, vt, tri, hg, x, a, kt, vm, w_out, gc, w_cq, w_co, gf, w1, w2, gfin, *, ff_chunk=1024):
    b, s, d = x.shape
    width = q.shape[-1]
    pairs = width // LANES
    m = vm.shape[1]
    tile_rows = SB_QBLOCKS * SB_BLOCK
    tm = SB_UNITS * tile_rows
    assert s % tm == 0
    nt = s // tm
    n = b * nt
    att = lambda g: jnp.minimum(g, n - 1)
    fin = lambda g: jnp.maximum(g - 1, 0)
    tile = lambda w, t: pl.BlockSpec((None, tm, w), lambda g: (t(g) // nt, t(g) % nt, 0))
    per_seq = lambda shape, t: pl.BlockSpec((None,) + shape, lambda g: (t(g) // nt,) + (0,) * len(shape))
    return pl.pallas_call(
        functools.partial(_sb_tail_kernel, tiles_per_seq=s // tile_rows, ff_chunk=ff_chunk),
        grid=(n + 1,),
        in_specs=[tile(width, att), per_seq((s, width), att), per_seq(vt.shape[1:], att),
                  _resident(tri.shape), _resident(hg.shape),
                  tile(d, fin), tile(GM_WIDTH, fin), per_seq((d, m), fin), per_seq((m, d), fin),
                  _resident(w_out.shape), _resident(gc.shape), _resident(w_cq.shape),
                  _resident(w_co.shape), _resident(gf.shape), _resident(w1.shape),
                  _resident(w2.shape), _resident(gfin.shape)],
        out_specs=tile(d, fin),
        out_shape=jax.ShapeDtypeStruct((b, s, d), x.dtype),
        scratch_shapes=[pltpu.VMEM((SB_UNITS, SB_QBLOCKS, pairs, LANES, 2 * SB_BLOCK), BF16),
                        pltpu.VMEM((2, SB_UNITS, SB_QBLOCKS, pairs, 1, 2 * SB_BLOCK), F32),
                        pltpu.VMEM((2, SB_UNITS, SB_QBLOCKS, pairs, LANES, SB_BLOCK), F32)],
        compiler_params=pltpu.CompilerParams(
            dimension_semantics=("arbitrary",), vmem_limit_bytes=VMEM_LIMIT),
        name="sb_tail",
    )(q, k, vt, tri, hg, x, a, kt, vm, w_out, gc, w_cq, w_co, gf, w1, w2, gfin)


def _suffix_sum_matrix():
    s = lax.broadcasted_iota(jnp.int32, (SB_BLOCK, SB_BLOCK), 0)
    j = lax.broadcasted_iota(jnp.int32, (SB_BLOCK, SB_BLOCK), 1)
    upper = (j > s).astype(BF16)
    return jnp.concatenate([upper, upper], axis=1)


def kernel(x, mem, norm_mix_g, w_in, gm_v_norm_g, w_spatial, b_spatial, head_norm_g, w_out,
           norm_cross_g, norm_mem_g, w_cq, w_ckv, w_co, norm_ffn_g, w_ff1, w_ff2, norm_final_g):
    assert w_in.shape[0] == 1, "single trunk layer only"
    tril = jnp.tril(jnp.ones((CHUNK, CHUNK), dtype=bool))
    row = lambda g: g.reshape(1, -1).astype(F32)
    wsp = jnp.where(tril[None], w_spatial[0], 0.0).astype(BF16)
    bsp = jnp.broadcast_to(b_spatial[0][:, :, None], (GM_GROUPS, CHUNK, GM_DIM)).astype(F32)
    hg = row(head_norm_g[0])
    a, q, k, vt = _mix_in(x, row(norm_mix_g[0]), w_in[0].astype(BF16), row(gm_v_norm_g[0]),
                          wsp, bsp, hg[:, :GM_WIDTH])
    kt, vm = _mem_kv(mem, row(norm_mem_g[0]), w_ckv[0].astype(BF16))
    return _sb_tail(q, k, vt, _suffix_sum_matrix(), hg[:, GM_WIDTH:], x, a, kt, vm,
                    w_out[0].astype(BF16), row(norm_cross_g[0]), w_cq[0].astype(BF16),
                    w_co[0].astype(BF16), row(norm_ffn_g[0]), w_ff1[0].astype(BF16),
                    w_ff2[0].astype(BF16), row(norm_final_g))
```

```python
import functools
import math

import jax
import jax.numpy as jnp
from jax import lax
from jax.experimental import pallas as pl
from jax.experimental.pallas import tpu as pltpu

EPS = 1e-6
CHUNK = 128
GM_GROUPS = 4
GM_DIM = 128
GM_WIDTH = GM_GROUPS * GM_DIM
SB_HEADS = 8
SB_HEAD_DIM = 64
SB_WIDTH = SB_HEADS * SB_HEAD_DIM
SB_BLOCK = 128
SB_QBLOCKS = 2
SB_UNITS = 2
X_HEADS = 4
MIX_PARTS = 2

LANES = 128
VMEM_LIMIT = 56 * 1024 * 1024

LOG2_ZERO = -151.0
LOG2_E = 1.4426950408889634

F32 = jnp.float32
BF16 = jnp.bfloat16


def _dot(a, b):
    return jnp.dot(a, b, preferred_element_type=F32)


def _inv_rms(x):
    return lax.rsqrt(jnp.mean(x * x, axis=-1, keepdims=True) + EPS)


def _rms(x, g):
    return x * _inv_rms(x) * g


def _resident(shape):
    zeros = (0,) * len(shape)
    return pl.BlockSpec(shape, lambda *_: zeros, pipeline_mode=pl.Buffered(1))


def _mix_in_kernel(x_ref, g_ref, win32_ref, gvg_ref, wsp_ref, bsp_ref, hg_ref,
                   a_ref, q_ref, k_ref, vt_ref, win_ref, *, tm):
    @pl.when(jnp.logical_and(pl.program_id(0) == 0, pl.program_id(1) == 0))
    def _():
        win_ref[...] = win32_ref[...].astype(BF16)

    base = 2 * GM_WIDTH
    scale = LOG2_E / math.sqrt(SB_HEAD_DIM)
    part = tm // MIX_PARTS
    for r0 in range(0, tm, part):
        x = x_ref[r0:r0 + part, :]
        xg = (x * g_ref[...]).astype(BF16)
        inv = _inv_rms(x)
        u = jax.nn.gelu(_dot(xg, win_ref[:, 0:GM_WIDTH]) * inv)
        gv = jax.nn.gelu(_dot(xg, win_ref[:, GM_WIDTH:2 * GM_WIDTH]) * inv)
        q_ref[r0:r0 + part, :] = (_dot(xg, win_ref[:, base:base + SB_WIDTH]) * (inv * scale)).astype(BF16)
        k_ref[r0:r0 + part, :] = (_dot(xg, win_ref[:, base + SB_WIDTH:base + 2 * SB_WIDTH]) * inv).astype(BF16)
        v = _dot(xg, win_ref[:, base + 2 * SB_WIDTH:base + 3 * SB_WIDTH]) * inv
        for g in range(GM_GROUPS):
            cols = slice(g * GM_DIM, (g + 1) * GM_DIM)
            gvn = _rms(gv[:, cols], gvg_ref[:, cols]).astype(BF16)
            for c in range(part // CHUNK):
                rows = slice(c * CHUNK, (c + 1) * CHUNK)
                mixed = _dot(wsp_ref[g], gvn[rows]) + bsp_ref[g]
                gated = u[rows, cols] * mixed
                a_ref[r0 + c * CHUNK:r0 + (c + 1) * CHUNK, cols] = _rms(gated, hg_ref[:, cols]).astype(BF16)
        for c in range(part // SB_BLOCK):
            vt_ref[r0 // SB_BLOCK + c] = v[c * SB_BLOCK:(c + 1) * SB_BLOCK, :].T.astype(BF16)


def _mix_in(x, g, w_in, gvg, wsp, bsp, hg, *, tm=1024):
    b, s, d = x.shape
    assert s % tm == 0
    tok = lambda width: pl.BlockSpec((None, tm, width), lambda i, j: (i, j, 0))
    out = jax.ShapeDtypeStruct((b, s, GM_WIDTH), BF16)
    vt_out = jax.ShapeDtypeStruct((b, s // SB_BLOCK, SB_WIDTH, SB_BLOCK), BF16)
    vt_spec = pl.BlockSpec((None, tm // SB_BLOCK, SB_WIDTH, SB_BLOCK), lambda i, j: (i, j, 0, 0))
    return pl.pallas_call(
        functools.partial(_mix_in_kernel, tm=tm),
        grid=(b, s // tm),
        in_specs=[tok(d), _resident(g.shape), _resident(w_in.shape), _resident(gvg.shape),
                  _resident(wsp.shape), _resident(bsp.shape), _resident(hg.shape)],
        out_specs=[tok(GM_WIDTH)] * 3 + [vt_spec],
        out_shape=[out] * 3 + [vt_out],
        scratch_shapes=[pltpu.VMEM(w_in.shape, BF16)],
        compiler_params=pltpu.CompilerParams(
            dimension_semantics=("arbitrary", "arbitrary"), vmem_limit_bytes=VMEM_LIMIT),
        name="mix_in",
    )(x, g, w_in, gvg, wsp, bsp, hg)


def _mem_kv_kernel(mem_ref, g_ref, w32_ref, kt_ref, v_ref, w_ref):
    @pl.when(pl.program_id(0) == 0)
    def _():
        w_ref[...] = w32_ref[...].astype(BF16)

    nb, m, d = mem_ref.shape
    mem = mem_ref[...].reshape(nb * m, d)
    mn = _rms(mem, g_ref[...]).astype(BF16)
    k = _dot(mn, w_ref[:, 0:d])
    for i in range(nb):
        kt_ref[i] = k[i * m:(i + 1) * m, :].T.astype(BF16)
    v_ref[...] = _dot(mn, w_ref[:, d:2 * d]).astype(BF16).reshape(nb, m, d)


def _mem_kv(mem, g, w_ckv, *, nb=4):
    b, m, d = mem.shape
    assert b % nb == 0
    return pl.pallas_call(
        _mem_kv_kernel,
        grid=(b // nb,),
        in_specs=[pl.BlockSpec((nb, m, d), lambda i: (i, 0, 0)),
                  _resident(g.shape), _resident(w_ckv.shape)],
        out_specs=[pl.BlockSpec((nb, d, m), lambda i: (i, 0, 0)),
                   pl.BlockSpec((nb, m, d), lambda i: (i, 0, 0))],
        out_shape=[jax.ShapeDtypeStruct((b, d, m), BF16), jax.ShapeDtypeStruct((b, m, d), BF16)],
        scratch_shapes=[pltpu.VMEM(w_ckv.shape, BF16)],
        compiler_params=pltpu.CompilerParams(
            dimension_semantics=("arbitrary",), vmem_limit_bytes=VMEM_LIMIT),
        name="mem_kv",
    )(mem, g, w_ckv)


def _tail_stages(x_ref, a_ref, sb, kt_ref, v_ref, wout_ref, gc_ref, wcq_ref, wco_ref, gf_ref, w1_ref,
                 w2_ref, gfin_ref, o_ref, ff_chunk):
    d = x_ref.shape[-1]
    hd = d // X_HEADS
    d_ff = w1_ref.shape[1]
    st = {}

    def attention():
        h = x_ref[...] + _dot(jnp.concatenate([a_ref[...], sb], axis=1), wout_ref[...])
        qx = _dot((h * gc_ref[...]).astype(BF16), wcq_ref[...])
        qx = (qx * (_inv_rms(h) * (1.0 / math.sqrt(hd)))).astype(BF16)
        heads = []
        for i in range(X_HEADS):
            cols = slice(i * hd, (i + 1) * hd)
            sc = _dot(qx[:, cols], kt_ref[cols, :])
            e = jnp.exp(sc - jnp.max(sc, axis=-1, keepdims=True))
            o = _dot(e.astype(BF16), v_ref[:, cols]) / jnp.sum(e, axis=-1, keepdims=True)
            heads.append(o.astype(BF16))
        st["h"] = h + _dot(jnp.concatenate(heads, axis=1), wco_ref[...])
        st["hg"] = (st["h"] * gf_ref[...]).astype(BF16)
        st["inv2"] = jnp.square(_inv_rms(st["h"]))
        st["ff"] = None

    def mlp_chunk(c):
        cols = slice(c * ff_chunk, (c + 1) * ff_chunk)
        r = jnp.maximum(_dot(st["hg"], w1_ref[:, cols]), 0.0)
        part = _dot((r * r).astype(BF16), w2_ref[cols, :])
        st["ff"] = part if st["ff"] is None else st["ff"] + part

    def finish():
        o_ref[...] = _rms(st["h"] + st["ff"] * st["inv2"], gfin_ref[...])

    return [attention] + [functools.partial(mlp_chunk, c) for c in range(d_ff // ff_chunk)] + [finish]


def _sb_tail_kernel(q_ref, k_ref, vt_ref, tri_ref, hg_ref,
                    x_ref, a_ref, kt_ref, vm_ref, wout_ref, gc_ref, wcq_ref, wco_ref, gf_ref,
                    w1_ref, w2_ref, gfin_ref, o_ref, qbd_ref, r_ref, acc_ref,
                    *, tiles_per_seq, ff_chunk):
    g = pl.program_id(0)
    last = pl.num_programs(0) - 2
    first_tile = jnp.minimum(g, last) * SB_UNITS
    slot = g % 2
    pairs = q_ref.shape[-1] // LANES
    tile_rows = SB_QBLOCKS * SB_BLOCK
    cols = [slice(p * LANES, (p + 1) * LANES) for p in range(pairs)]
    top_rows = lax.broadcasted_iota(jnp.int32, (LANES, SB_BLOCK), 0) < SB_HEAD_DIM
    s_idx = lax.broadcasted_iota(jnp.int32, (SB_BLOCK, 2 * SB_BLOCK), 0)
    t_idx = lax.broadcasted_iota(jnp.int32, (SB_BLOCK, 2 * SB_BLOCK), 1) % SB_BLOCK
    strictly_causal = s_idx < t_idx
    every = range(SB_QBLOCKS)

    @pl.when(g == 0)
    def _():
        acc_ref[...] = jnp.zeros_like(acc_ref)

    def block_diag_heads(x):
        zero = jnp.zeros_like(x)
        return jnp.concatenate([jnp.where(top_rows, x, zero), jnp.where(top_rows, zero, x)], axis=1)

    def log_weights(z, masked):
        log_beta, first_row, hl = [], [], []
        for zc in z:
            lb = jnp.minimum(zc, 0.0) - jnp.log2(1.0 + jnp.exp2(-jnp.abs(zc)))
            lm = lb - zc
            if masked:
                lm = jnp.where(strictly_causal, lm, 0.0)
            hi = lm.astype(BF16)
            lo = (lm - hi.astype(F32)).astype(BF16)
            log_beta.append(lb)
            first_row.append(lm[0:1, :])
            hl.append(jnp.concatenate([hi, lo], axis=0))
        suffix = [_dot(tri_ref[...], x) for x in hl]
        total = [sf[0:1, :] + fr for sf, fr in zip(suffix, first_row)]
        return log_beta, suffix, total

    def weights(chains, log_beta, suffix, total, r, masked):
        a_t = []
        for i, (c, j, ok, p) in enumerate(chains):
            r_in = r[c, p] if ok is True else jnp.where(ok, r[c, p], -jnp.inf)
            a = jnp.exp2(log_beta[i] + suffix[i] + r_in)
            if masked:
                a = jnp.where(strictly_causal, a, 0.0)
            a = a.astype(BF16)
            a_t.append(jnp.concatenate([a[:, :SB_BLOCK], a[:, SB_BLOCK:]], axis=0))
            r[c, p] = r[c, p] + (total[i] if ok is True else jnp.where(ok, total[i], 0.0))
        return a_t

    def values(j, p, a_ts):
        v_bd = block_diag_heads(vt_ref[j, cols[p], :])
        pv = _dot(v_bd, a_ts[0] if len(a_ts) == 1 else jnp.concatenate(a_ts, axis=1))
        return [pv[:, i * SB_BLOCK:(i + 1) * SB_BLOCK] for i in range(len(a_ts))]

    def attend_tile(u):
        base = ((first_tile + u) % tiles_per_seq) * SB_QBLOCKS
        row0 = u * tile_rows

        for c in every:
            q_t = q_ref[row0 + c * SB_BLOCK:row0 + (c + 1) * SB_BLOCK, :].astype(F32).T
            for p in range(pairs):
                qbd_ref[u, c, p] = block_diag_heads(q_t[cols[p], :].astype(BF16))

        def scores(blocks):
            chains = [(c, j, ok, p) for c, j, ok in blocks for p in range(pairs)]
            z = [_dot(k_ref[pl.ds(pl.multiple_of(j * SB_BLOCK, SB_BLOCK), SB_BLOCK), cols[p]],
                      qbd_ref[u, c, p]) for c, j, ok, p in chains]
            return chains, z

        def sweep(waves, fresh=False, fillers=()):
            r = {}
            for blocks, _ in waves:
                for c, _, _ in blocks:
                    for p in range(pairs):
                        if (c, p) not in r:
                            r[c, p] = (jnp.zeros((1, 2 * SB_BLOCK), F32) if fresh
                                       else r_ref[slot, u, c, p])
            out = {}

            def add(c, p, x):
                out[c, p] = x if (c, p) not in out else out[c, p] + x

            staged, waiting = [], {}
            for w in range(len(waves) + 3):
                if w < len(waves):
                    blocks, masked = waves[w]
                    chains, z = scores(blocks)
                    staged.append([chains, z, masked])
                if 1 <= w <= len(waves):
                    item = staged[w - 1]
                    item[1:2] = [log_weights(item[1], item[2])]
                if 2 <= w:
                    done = w - 2
                    if done < len(waves):
                        chains, (log_beta, suffix, total), masked = staged[done]
                        a_t = weights(chains, log_beta, suffix, total, r, masked)
                    else:
                        chains, a_t = [], []
                    for (c, j, ok, p), a in zip(chains, a_t):
                        partner = waiting.pop((c - 1, done - 1, p), None)
                        if partner is not None:
                            first, second = values(j, p, [partner[1], a])
                            add(c - 1, p, first)
                            add(c, p, second)
                        elif c + 1 < SB_QBLOCKS and done + 1 < len(waves):
                            waiting[c, done, p] = (j, a)
                        else:
                            add(c, p, values(j, p, [a])[0])
                    for (c, d, p) in [key for key in waiting if key[1] < done]:
                        j, a = waiting.pop((c, d, p))
                        add(c, p, values(j, p, [a])[0])
                if w < len(waves) + 2:
                    for filler in fillers[w:w + 1]:
                        filler()
            bound = None
            for (c, p), x in out.items():
                acc_ref[slot, u, c, p] = x if fresh else acc_ref[slot, u, c, p] + x
                r_ref[slot, u, c, p] = r[c, p]
                bound = r[c, p] if bound is None else jnp.maximum(bound, r[c, p])
            bound = jnp.max(bound)
            for filler in fillers[len(waves) + 2:]:
                filler()
            return bound

        def left(m):
            return ([(c, jnp.maximum(base + c - m, 0), base + c - m >= 0) for c in every], False)

        diagonal = ([(c, base + c, True) for c in every], True)
        return sweep, diagonal, left, base

    def finished_attention(u):
        rows = []
        for c in every:
            row = []
            for p in range(pairs):
                acc = acc_ref[1 - slot, u, c, p]
                sq = acc * acc
                ms0 = jnp.sum(sq[:SB_HEAD_DIM], axis=0, keepdims=True)
                ms1 = jnp.sum(sq[SB_HEAD_DIM:], axis=0, keepdims=True)
                ms = jnp.where(top_rows, ms0, ms1) * (1.0 / SB_HEAD_DIM)
                normed = (acc * lax.rsqrt(ms + EPS)).T
                row.append((normed * hg_ref[:, cols[p]]).astype(BF16))
            rows.append(jnp.concatenate(row, axis=1))
        return jnp.concatenate(rows, axis=0)

    pending = []
    for u in range(SB_UNITS):
        rows = slice(u * tile_rows, (u + 1) * tile_rows)
        tail = _tail_stages(x_ref.at[rows, :], a_ref.at[rows, :], finished_attention(u), kt_ref, vm_ref,
                            wout_ref, gc_ref, wcq_ref, wco_ref, gf_ref, w1_ref, w2_ref, gfin_ref,
                            o_ref.at[rows, :], ff_chunk)
        sweep, diagonal, left, base = attend_tile(u)
        pending.append((sweep, left, base, sweep([diagonal, left(1), left(2)], fresh=True, fillers=tail)))

    for sweep, left, base, bound in pending:
        def cond(carry, base=base):
            m, bound = carry
            return jnp.logical_and(m < base + SB_QBLOCKS, bound >= LOG2_ZERO)

        def body(carry, sweep=sweep, left=left):
            m, _ = carry
            return m + 1, sweep([left(m)])

        lax.while_loop(cond, body, (jnp.int32(3), bound))


def _sb_tail(q, k, vt, tri, hg, x, a, kt, vm, w_out, gc, w_cq, w_co, gf, w1, w2, gfin, *, ff_chunk=1024):
    b, s, d = x.shape
    width = q.shape[-1]
    pairs = width // LANES
    m = vm.shape[1]
    tile_rows = SB_QBLOCKS * SB_BLOCK
    tm = SB_UNITS * tile_rows
    assert s % tm == 0
    nt = s // tm
    n = b * nt
    att = lambda g: jnp.minimum(g, n - 1)
    fin = lambda g: jnp.maximum(g - 1, 0)
    tile = lambda w, t: pl.BlockSpec((None, tm, w), lambda g: (t(g) // nt, t(g) % nt, 0))
    per_seq = lambda shape, t: pl.BlockSpec((None,) + shape, lambda g: (t(g) // nt,) + (0,) * len(shape))
    return pl.pallas_call(
        functools.partial(_sb_tail_kernel, tiles_per_seq=s // tile_rows, ff_chunk=ff_chunk),
        grid=(n + 1,),
        in_specs=[tile(width, att), per_seq((s, width), att), per_seq(vt.shape[1:], att),
                  _resident(tri.shape), _resident(hg.shape),
                  tile(d, fin), tile(GM_WIDTH, fin), per_seq((d, m), fin), per_seq((m, d), fin),
                  _resident(w_out.shape), _resident(gc.shape), _resident(w_cq.shape),
                  _resident(w_co.shape), _resident(gf.shape), _resident(w1.shape),
                  _resident(w2.shape), _resident(gfin.shape)],
        out_specs=tile(d, fin),
        out_shape=jax.ShapeDtypeStruct((b, s, d), x.dtype),
        scratch_shapes=[pltpu.VMEM((SB_UNITS, SB_QBLOCKS, pairs, LANES, 2 * SB_BLOCK), BF16),
                        pltpu.VMEM((2, SB_UNITS, SB_QBLOCKS, pairs, 1, 2 * SB_BLOCK), F32),
                        pltpu.VMEM((2, SB_UNITS, SB_QBLOCKS, pairs, LANES, SB_BLOCK), F32)],
        compiler_params=pltpu.CompilerParams(
            dimension_semantics=("arbitrary",), vmem_limit_bytes=VMEM_LIMIT),
        name="sb_tail",
    )(q, k, vt, tri, hg, x, a, kt, vm, w_out, gc, w_cq, w_co, gf, w1, w2, gfin)


def _suffix_sum_matrix():
    s = lax.broadcasted_iota(jnp.int32, (SB_BLOCK, SB_BLOCK), 0)
    j = lax.broadcasted_iota(jnp.int32, (SB_BLOCK, SB_BLOCK), 1)
    upper = (j > s).astype(BF16)
    return jnp.concatenate([upper, upper], axis=1)


def kernel(x, mem, norm_mix_g, w_in, gm_v_norm_g, w_spatial, b_spatial, head_norm_g, w_out,
           norm_cross_g, norm_mem_g, w_cq, w_ckv, w_co, norm_ffn_g, w_ff1, w_ff2, norm_final_g):
    assert w_in.shape[0] == 1, "single trunk layer only"
    tril = jnp.tril(jnp.ones((CHUNK, CHUNK), dtype=bool))
    row = lambda g: g.reshape(1, -1).astype(F32)
    wsp = jnp.where(tril[None], w_spatial[0], 0.0).astype(BF16)
    bsp = jnp.broadcast_to(b_spatial[0][:, :, None], (GM_GROUPS, CHUNK, GM_DIM)).astype(F32)
    hg = row(head_norm_g[0])
    a, q, k, vt = _mix_in(x, row(norm_mix_g[0]), w_in[0], row(gm_v_norm_g[0]),
                          wsp, bsp, hg[:, :GM_WIDTH])
    kt, vm = _mem_kv(mem, row(norm_mem_g[0]), w_ckv[0])
    return _sb_tail(q, k, vt, _suffix_sum_matrix(), hg[:, GM_WIDTH:], x, a, kt, vm,
                    w_out[0].astype(BF16), row(norm_cross_g[0]), w_cq[0].astype(BF16),
                    w_co[0].astype(BF16), row(norm_ffn_g[0]), w_ff1[0].astype(BF16),
                    w_ff2[0].astype(BF16), row(norm_final_g))
```

```python
import functools
import math

import jax
import jax.numpy as jnp
from jax import lax
from jax.experimental import pallas as pl
from jax.experimental.pallas import tpu as pltpu

EPS = 1e-6
CHUNK = 128
GM_GROUPS = 4
GM_DIM = 128
GM_WIDTH = GM_GROUPS * GM_DIM
SB_HEADS = 8
SB_HEAD_DIM = 64
SB_WIDTH = SB_HEADS * SB_HEAD_DIM
SB_BLOCK = 128
SB_QBLOCKS = 2
SB_UNITS = 2
X_HEADS = 4
MIX_PARTS = 2

LANES = 128
VMEM_LIMIT = 56 * 1024 * 1024

LOG2_ZERO = -151.0
LOG2_E = 1.4426950408889634

F32 = jnp.float32
BF16 = jnp.bfloat16


def _dot(a, b):
    return jnp.dot(a, b, preferred_element_type=F32)


def _inv_rms(x):
    return lax.rsqrt(jnp.mean(x * x, axis=-1, keepdims=True) + EPS)


def _rms(x, g):
    return x * _inv_rms(x) * g


def _resident(shape):
    zeros = (0,) * len(shape)
    return pl.BlockSpec(shape, lambda *_: zeros, pipeline_mode=pl.Buffered(1))


def _mix_in_kernel(x_ref, g_ref, win_ref, gvg_ref, wsp_ref, bsp_ref, hg_ref,
                   a_ref, q_ref, k_ref, vt_ref, *, tm):
    base = 2 * GM_WIDTH
    scale = LOG2_E / math.sqrt(SB_HEAD_DIM)
    part = tm // MIX_PARTS
    for r0 in range(0, tm, part):
        x = x_ref[r0:r0 + part, :]
        xg = (x * g_ref[...]).astype(BF16)
        inv = _inv_rms(x)
        u = jax.nn.gelu(_dot(xg, win_ref[:, 0:GM_WIDTH]) * inv)
        gv = jax.nn.gelu(_dot(xg, win_ref[:, GM_WIDTH:2 * GM_WIDTH]) * inv)
        q_ref[r0:r0 + part, :] = (_dot(xg, win_ref[:, base:base + SB_WIDTH]) * (inv * scale)).astype(BF16)
        k_ref[r0:r0 + part, :] = (_dot(xg, win_ref[:, base + SB_WIDTH:base + 2 * SB_WIDTH]) * inv).astype(BF16)
        v = _dot(xg, win_ref[:, base + 2 * SB_WIDTH:base + 3 * SB_WIDTH]) * inv
        for g in range(GM_GROUPS):
            cols = slice(g * GM_DIM, (g + 1) * GM_DIM)
            gvn = _rms(gv[:, cols], gvg_ref[:, cols]).astype(BF16)
            for c in range(part // CHUNK):
                rows = slice(c * CHUNK, (c + 1) * CHUNK)
                mixed = _dot(wsp_ref[g], gvn[rows]) + bsp_ref[g]
                gated = u[rows, cols] * mixed
                a_ref[r0 + c * CHUNK:r0 + (c + 1) * CHUNK, cols] = _rms(gated, hg_ref[:, cols]).astype(BF16)
        for c in range(part // SB_BLOCK):
            vt_ref[r0 // SB_BLOCK + c] = v[c * SB_BLOCK:(c + 1) * SB_BLOCK, :].T.astype(BF16)


def _mix_in(x, g, w_in, gvg, wsp, bsp, hg, *, tm=1024):
    b, s, d = x.shape
    assert s % tm == 0
    tok = lambda width: pl.BlockSpec((None, tm, width), lambda i, j: (i, j, 0))
    out = jax.ShapeDtypeStruct((b, s, GM_WIDTH), BF16)
    vt_out = jax.ShapeDtypeStruct((b, s // SB_BLOCK, SB_WIDTH, SB_BLOCK), BF16)
    vt_spec = pl.BlockSpec((None, tm // SB_BLOCK, SB_WIDTH, SB_BLOCK), lambda i, j: (i, j, 0, 0))
    return pl.pallas_call(
        functools.partial(_mix_in_kernel, tm=tm),
        grid=(b, s // tm),
        in_specs=[tok(d), _resident(g.shape), _resident(w_in.shape), _resident(gvg.shape),
                  _resident(wsp.shape), _resident(bsp.shape), _resident(hg.shape)],
        out_specs=[tok(GM_WIDTH)] * 3 + [vt_spec],
        out_shape=[out] * 3 + [vt_out],
        compiler_params=pltpu.CompilerParams(
            dimension_semantics=("parallel", "parallel"), vmem_limit_bytes=VMEM_LIMIT,
            allow_input_fusion=[False, False, True, False, False, False, False]),
        name="mix_in",
    )(x, g, w_in, gvg, wsp, bsp, hg)


def _mem_kv_kernel(mem_ref, g_ref, w_ref, kt_ref, v_ref):
    nb, m, d = mem_ref.shape
    mem = mem_ref[...].reshape(nb * m, d)
    mn = _rms(mem, g_ref[...]).astype(BF16)
    k = _dot(mn, w_ref[:, 0:d])
    for i in range(nb):
        kt_ref[i] = k[i * m:(i + 1) * m, :].T.astype(BF16)
    v_ref[...] = _dot(mn, w_ref[:, d:2 * d]).astype(BF16).reshape(nb, m, d)


def _mem_kv(mem, g, w_ckv, *, nb=4):
    b, m, d = mem.shape
    assert b % nb == 0
    return pl.pallas_call(
        _mem_kv_kernel,
        grid=(b // nb,),
        in_specs=[pl.BlockSpec((nb, m, d), lambda i: (i, 0, 0)),
                  _resident(g.shape), _resident(w_ckv.shape)],
        out_specs=[pl.BlockSpec((nb, d, m), lambda i: (i, 0, 0)),
                   pl.BlockSpec((nb, m, d), lambda i: (i, 0, 0))],
        out_shape=[jax.ShapeDtypeStruct((b, d, m), BF16), jax.ShapeDtypeStruct((b, m, d), BF16)],
        compiler_params=pltpu.CompilerParams(
            dimension_semantics=("parallel",), vmem_limit_bytes=VMEM_LIMIT,
            allow_input_fusion=[False, False, True]),
        name="mem_kv",
    )(mem, g, w_ckv)


def _tail_stages(x_ref, a_ref, sb, kt_ref, v_ref, wout_ref, gc_ref, wcq_ref, wco_ref, gf_ref, w1_ref,
                 w2_ref, gfin_ref, o_ref, ff_chunk):
    d = x_ref.shape[-1]
    hd = d // X_HEADS
    d_ff = w1_ref.shape[1]
    st = {}

    def attention():
        h = x_ref[...] + _dot(jnp.concatenate([a_ref[...], sb], axis=1), wout_ref[...])
        qx = _dot((h * gc_ref[...]).astype(BF16), wcq_ref[...])
        qx = (qx * (_inv_rms(h) * (1.0 / math.sqrt(hd)))).astype(BF16)
        heads = []
        for i in range(X_HEADS):
            cols = slice(i * hd, (i + 1) * hd)
            sc = _dot(qx[:, cols], kt_ref[cols, :])
            e = jnp.exp(sc - jnp.max(sc, axis=-1, keepdims=True))
            o = _dot(e.astype(BF16), v_ref[:, cols]) / jnp.sum(e, axis=-1, keepdims=True)
            heads.append(o.astype(BF16))
        st["h"] = h + _dot(jnp.concatenate(heads, axis=1), wco_ref[...])
        st["hg"] = (st["h"] * gf_ref[...]).astype(BF16)
        st["inv2"] = jnp.square(_inv_rms(st["h"]))
        st["ff"] = None

    def mlp_chunk(c):
        cols = slice(c * ff_chunk, (c + 1) * ff_chunk)
        r = jnp.maximum(_dot(st["hg"], w1_ref[:, cols]), 0.0)
        part = _dot((r * r).astype(BF16), w2_ref[cols, :])
        st["ff"] = part if st["ff"] is None else st["ff"] + part

    def finish():
        o_ref[...] = _rms(st["h"] + st["ff"] * st["inv2"], gfin_ref[...])

    return [attention] + [functools.partial(mlp_chunk, c) for c in range(d_ff // ff_chunk)] + [finish]


def _sb_tail_kernel(q_ref, k_ref, vt_ref, tri_ref, hg_ref,
                    x_ref, a_ref, kt_ref, vm_ref, wout_ref, gc_ref, wcq_ref, wco_ref, gf_ref,
                    w1_ref, w2_ref, gfin_ref, o_ref, qbd_ref, r_ref, acc_ref,
                    *, tiles_per_seq, ff_chunk):
    g = pl.program_id(0)
    last = pl.num_programs(0) - 2
    first_tile = jnp.minimum(g, last) * SB_UNITS
    slot = g % 2
    pairs = q_ref.shape[-1] // LANES
    tile_rows = SB_QBLOCKS * SB_BLOCK
    cols = [slice(p * LANES, (p + 1) * LANES) for p in range(pairs)]
    top_rows = lax.broadcasted_iota(jnp.int32, (LANES, SB_BLOCK), 0) < SB_HEAD_DIM
    s_idx = lax.broadcasted_iota(jnp.int32, (SB_BLOCK, 2 * SB_BLOCK), 0)
    t_idx = lax.broadcasted_iota(jnp.int32, (SB_BLOCK, 2 * SB_BLOCK), 1) % SB_BLOCK
    strictly_causal = s_idx < t_idx
    every = range(SB_QBLOCKS)

    @pl.when(g == 0)
    def _():
        acc_ref[...] = jnp.zeros_like(acc_ref)

    def block_diag_heads(x):
        zero = jnp.zeros_like(x)
        return jnp.concatenate([jnp.where(top_rows, x, zero), jnp.where(top_rows, zero, x)], axis=1)

    def log_weights(z, masked):
        log_beta, first_row, hl = [], [], []
        for zc in z:
            lb = jnp.minimum(zc, 0.0) - jnp.log2(1.0 + jnp.exp2(-jnp.abs(zc)))
            lm = lb - zc
            if masked:
                lm = jnp.where(strictly_causal, lm, 0.0)
            hi = lm.astype(BF16)
            lo = (lm - hi.astype(F32)).astype(BF16)
            log_beta.append(lb)
            first_row.append(lm[0:1, :])
            hl.append(jnp.concatenate([hi, lo], axis=0))
        suffix = [_dot(tri_ref[...], x) for x in hl]
        total = [sf[0:1, :] + fr for sf, fr in zip(suffix, first_row)]
        return log_beta, suffix, total

    def weights(chains, log_beta, suffix, total, r, masked):
        a_t = []
        for i, (c, j, ok, p) in enumerate(chains):
            r_in = r[c, p] if ok is True else jnp.where(ok, r[c, p], -jnp.inf)
            a = jnp.exp2(log_beta[i] + suffix[i] + r_in)
            if masked:
                a = jnp.where(strictly_causal, a, 0.0)
            a = a.astype(BF16)
            a_t.append(jnp.concatenate([a[:, :SB_BLOCK], a[:, SB_BLOCK:]], axis=0))
            r[c, p] = r[c, p] + (total[i] if ok is True else jnp.where(ok, total[i], 0.0))
        return a_t

    def values(j, p, a_ts):
        v_bd = block_diag_heads(vt_ref[j, cols[p], :])
        pv = _dot(v_bd, a_ts[0] if len(a_ts) == 1 else jnp.concatenate(a_ts, axis=1))
        return [pv[:, i * SB_BLOCK:(i + 1) * SB_BLOCK] for i in range(len(a_ts))]

    def attend_tile(u):
        base = ((first_tile + u) % tiles_per_seq) * SB_QBLOCKS
        row0 = u * tile_rows

        for c in every:
            q_t = q_ref[row0 + c * SB_BLOCK:row0 + (c + 1) * SB_BLOCK, :].astype(F32).T
            for p in range(pairs):
                qbd_ref[u, c, p] = block_diag_heads(q_t[cols[p], :].astype(BF16))

        def scores(blocks):
            chains = [(c, j, ok, p) for c, j, ok in blocks for p in range(pairs)]
            z = [_dot(k_ref[pl.ds(pl.multiple_of(j * SB_BLOCK, SB_BLOCK), SB_BLOCK), cols[p]],
                      qbd_ref[u, c, p]) for c, j, ok, p in chains]
            return chains, z

        def sweep(waves, fresh=False, fillers=()):
            r = {}
            for blocks, _ in waves:
                for c, _, _ in blocks:
                    for p in range(pairs):
                        if (c, p) not in r:
                            r[c, p] = (jnp.zeros((1, 2 * SB_BLOCK), F32) if fresh
                                       else r_ref[slot, u, c, p])
            out = {}

            def add(c, p, x):
                out[c, p] = x if (c, p) not in out else out[c, p] + x

            staged, waiting = [], {}
            for w in range(len(waves) + 3):
                if w < len(waves):
                    blocks, masked = waves[w]
                    chains, z = scores(blocks)
                    staged.append([chains, z, masked])
                if 1 <= w <= len(waves):
                    item = staged[w - 1]
                    item[1:2] = [log_weights(item[1], item[2])]
                if 2 <= w:
                    done = w - 2
                    if done < len(waves):
                        chains, (log_beta, suffix, total), masked = staged[done]
                        a_t = weights(chains, log_beta, suffix, total, r, masked)
                    else:
                        chains, a_t = [], []
                    for (c, j, ok, p), a in zip(chains, a_t):
                        partner = waiting.pop((c - 1, done - 1, p), None)
                        if partner is not None:
                            first, second = values(j, p, [partner[1], a])
                            add(c - 1, p, first)
                            add(c, p, second)
                        elif c + 1 < SB_QBLOCKS and done + 1 < len(waves):
                            waiting[c, done, p] = (j, a)
                        else:
                            add(c, p, values(j, p, [a])[0])
                    for (c, d, p) in [key for key in waiting if key[1] < done]:
                        j, a = waiting.pop((c, d, p))
                        add(c, p, values(j, p, [a])[0])
                if w < len(waves) + 2:
                    for filler in fillers[w:w + 1]:
                        filler()
            bound = None
            for (c, p), x in out.items():
                acc_ref[slot, u, c, p] = x if fresh else acc_ref[slot, u, c, p] + x
                r_ref[slot, u, c, p] = r[c, p]
                bound = r[c, p] if bound is None else jnp.maximum(bound, r[c, p])
            bound = jnp.max(bound)
            for filler in fillers[len(waves) + 2:]:
                filler()
            return bound

        def left(m):
            return ([(c, jnp.maximum(base + c - m, 0), base + c - m >= 0) for c in every], False)

        diagonal = ([(c, base + c, True) for c in every], True)
        return sweep, diagonal, left, base

    def finished_attention(u):
        rows = []
        for c in every:
            row = []
            for p in range(pairs):
                acc = acc_ref[1 - slot, u, c, p]
                sq = acc * acc
                ms0 = jnp.sum(sq[:SB_HEAD_DIM], axis=0, keepdims=True)
                ms1 = jnp.sum(sq[SB_HEAD_DIM:], axis=0, keepdims=True)
                ms = jnp.where(top_rows, ms0, ms1) * (1.0 / SB_HEAD_DIM)
                normed = (acc * lax.rsqrt(ms + EPS)).T
                row.append((normed * hg_ref[:, cols[p]]).astype(BF16))
            rows.append(jnp.concatenate(row, axis=1))
        return jnp.concatenate(rows, axis=0)

    pending = []
    for u in range(SB_UNITS):
        rows = slice(u * tile_rows, (u + 1) * tile_rows)
        tail = _tail_stages(x_ref.at[rows, :], a_ref.at[rows, :], finished_attention(u), kt_ref, vm_ref,
                            wout_ref, gc_ref, wcq_ref, wco_ref, gf_ref, w1_ref, w2_ref, gfin_ref,
                            o_ref.at[rows, :], ff_chunk)
        sweep, diagonal, left, base = attend_tile(u)
        pending.append((sweep, left, base, sweep([diagonal, left(1), left(2)], fresh=True, fillers=tail)))

    for sweep, left, base, bound in pending:
        def cond(carry, base=base):
            m, bound = carry
            return jnp.logical_and(m < base + SB_QBLOCKS, bound >= LOG2_ZERO)

        def body(carry, sweep=sweep, left=left):
            m, _ = carry
            return m + 1, sweep([left(m)])

        lax.while_loop(cond, body, (jnp.int32(3), bound))


def _sb_tail(q, k, vt, tri, hg, x, a, kt, vm, w_out, gc, w_cq, w_co, gf, w1, w2, gfin, *, ff_chunk=1024):
    b, s, d = x.shape
    width = q.shape[-1]
    pairs = width // LANES
    m = vm.shape[1]
    tile_rows = SB_QBLOCKS * SB_BLOCK
    tm = SB_UNITS * tile_rows
    assert s % tm == 0
    nt = s // tm
    n = b * nt
    att = lambda g: jnp.minimum(g, n - 1)
    fin = lambda g: jnp.maximum(g - 1, 0)
    tile = lambda w, t: pl.BlockSpec((None, tm, w), lambda g: (t(g) // nt, t(g) % nt, 0))
    per_seq = lambda shape, t: pl.BlockSpec((None,) + shape, lambda g: (t(g) // nt,) + (0,) * len(shape))
    return pl.pallas_call(
        functools.partial(_sb_tail_kernel, tiles_per_seq=s // tile_rows, ff_chunk=ff_chunk),
        grid=(n + 1,),
        in_specs=[tile(width, att), per_seq((s, width), att), per_seq(vt.shape[1:], att),
                  _resident(tri.shape), _resident(hg.shape),
                  tile(d, fin), tile(GM_WIDTH, fin), per_seq((d, m), fin), per_seq((m, d), fin),
                  _resident(w_out.shape), _resident(gc.shape), _resident(w_cq.shape),
                  _resident(w_co.shape), _resident(gf.shape), _resident(w1.shape),
                  _resident(w2.shape), _resident(gfin.shape)],
        out_specs=tile(d, fin),
        out_shape=jax.ShapeDtypeStruct((b, s, d), x.dtype),
        scratch_shapes=[pltpu.VMEM((SB_UNITS, SB_QBLOCKS, pairs, LANES, 2 * SB_BLOCK), BF16),
                        pltpu.VMEM((2, SB_UNITS, SB_QBLOCKS, pairs, 1, 2 * SB_BLOCK), F32),
                        pltpu.VMEM((2, SB_UNITS, SB_QBLOCKS, pairs, LANES, SB_BLOCK), F32)],
        compiler_params=pltpu.CompilerParams(
            dimension_semantics=("arbitrary",), vmem_limit_bytes=VMEM_LIMIT),
        name="sb_tail",
    )(q, k, vt, tri, hg, x, a, kt, vm, w_out, gc, w_cq, w_co, gf, w1, w2, gfin)


def _suffix_sum_matrix():
    s = lax.broadcasted_iota(jnp.int32, (SB_BLOCK, SB_BLOCK), 0)
    j = lax.broadcasted_iota(jnp.int32, (SB_BLOCK, SB_BLOCK), 1)
    upper = (j > s).astype(BF16)
    return jnp.concatenate([upper, upper], axis=1)


def kernel(x, mem, norm_mix_g, w_in, gm_v_norm_g, w_spatial, b_spatial, head_norm_g, w_out,
           norm_cross_g, norm_mem_g, w_cq, w_ckv, w_co, norm_ffn_g, w_ff1, w_ff2, norm_final_g):
    assert w_in.shape[0] == 1, "single trunk layer only"
    tril = jnp.tril(jnp.ones((CHUNK, CHUNK), dtype=bool))
    row = lambda g: g.reshape(1, -1).astype(F32)
    wsp = jnp.where(tril[None], w_spatial[0], 0.0).astype(BF16)
    bsp = jnp.broadcast_to(b_spatial[0][:, :, None], (GM_GROUPS, CHUNK, GM_DIM)).astype(F32)
    hg = row(head_norm_g[0])
    a, q, k, vt = _mix_in(x, row(norm_mix_g[0]), w_in[0].astype(BF16), row(gm_v_norm_g[0]),
                          wsp, bsp, hg[:, :GM_WIDTH])
    kt, vm = _mem_kv(mem, row(norm_mem_g[0]), w_ckv[0].astype(BF16))
    return _sb_tail(q, k, vt, _suffix_sum_matrix(), hg[:, GM_WIDTH:], x, a, kt, vm,
                    w_out[0].astype(BF16), row(norm_cross_g[0]), w_cq[0].astype(BF16),
                    w_co[0].astype(BF16), row(norm_ffn_g[0]), w_ff1[0].astype(BF16),
                    w_ff2[0].astype(BF16), row(norm_final_g))
```
